```python
import jax, jax.numpy as jnp
from jax import lax
import numpy as np

D_MODEL = 2048
BATCH = 4
SEQ = 2048
DEPTH = 4

GRID_W = 64
CTX_LEN = 256
N_MIXERS = 2
EXPAND = 2
D_BRANCH = EXPAND * D_MODEL
RET_HEADS = 8
RET_QK_DIM = D_MODEL // RET_HEADS
RET_V_DIM = D_BRANCH // RET_HEADS
RET_CHUNK = 128
NA_HEADS = 32
NA_HEAD_DIM = D_BRANCH // NA_HEADS
NA_WIN_H = 8
NA_WIN_W = 16
NA_QBLOCK_W = 16
ROPE_BASE = 10000.0
EPS = 1e-6

kernel_name = 'hybrid_retention_natten_prefix_dit'


def rmsnorm(x, g):
    xf = x.astype(jnp.float32)
    y = xf * lax.rsqrt(jnp.mean(xf * xf, axis=-1, keepdims=True) + EPS)
    return (y * g.astype(jnp.float32)).astype(x.dtype)


def head_norm(x):
    xf = x.astype(jnp.float32)
    return (xf * lax.rsqrt(jnp.mean(xf * xf, axis=-1, keepdims=True) + EPS)).astype(x.dtype)


def to_heads(a, n_heads):
    b, t, _ = a.shape
    return a.reshape(b, t, n_heads, -1).transpose(0, 2, 1, 3)


def from_heads(a):
    b, h, t, d = a.shape
    return a.transpose(0, 2, 1, 3).reshape(b, t, h * d)


def rope_1d(x, pos):
    half = x.shape[-1] // 2
    freqs = ROPE_BASE ** (-jnp.arange(half, dtype=jnp.float32) / half)
    ang = pos.astype(jnp.float32)[:, None] * freqs[None, :]
    cos = jnp.cos(ang).astype(x.dtype)
    sin = jnp.sin(ang).astype(x.dtype)
    x1, x2 = x[..., :half], x[..., half:]
    return jnp.concatenate([x1 * cos - x2 * sin, x1 * sin + x2 * cos], axis=-1)


def rope_2d(x, row, col):
    a = x.shape[-1] // 2
    return jnp.concatenate([rope_1d(x[..., :a], row), rope_1d(x[..., a:], col)], axis=-1)


def retention_scan(q, k, v, log_g, s0):
    b, h, t, dk = q.shape
    dv = v.shape[-1]
    n = t // RET_CHUNK
    pos = jnp.arange(RET_CHUNK, dtype=jnp.float32)
    diff = pos[:, None] - pos[None, :]
    lg = log_g[:, None, None]
    intra = jnp.where(diff >= 0, jnp.exp(lg * jnp.maximum(diff, 0.0)), 0.0).astype(q.dtype)
    q_dec = jnp.exp(log_g[:, None] * (pos + 1.0))[..., None].astype(q.dtype)
    k_dec = jnp.exp(log_g[:, None] * (RET_CHUNK - 1.0 - pos))[..., None].astype(q.dtype)
    c_dec = jnp.exp(log_g * RET_CHUNK)[:, None, None].astype(q.dtype)

    def chunks(a):
        return a.reshape(b, h, n, RET_CHUNK, a.shape[-1]).transpose(2, 0, 1, 3, 4)

    def step(s, inp):
        qc, kc, vc = inp
        scores = jnp.einsum('bhid,bhjd->bhij', qc, kc) * intra
        o = jnp.einsum('bhij,bhjv->bhiv', scores, vc) + jnp.einsum('bhid,bhdv->bhiv', qc * q_dec, s)
        s_new = s * c_dec + jnp.einsum('bhjd,bhjv->bhdv', kc * k_dec, vc)
        return s_new.astype(s.dtype), o

    s_fin, o = lax.scan(step, s0, (chunks(q), chunks(k), chunks(v)))
    return o.transpose(1, 2, 0, 3, 4).reshape(b, h, t, dv), s_fin


def retention_branch(h, hc, w_in, raw_fwd, raw_bwd, w_out, row_pos, col_pos, with_ctx_out):
    b = h.shape[0]
    cut = [D_MODEL, 2 * D_MODEL, 2 * D_MODEL + D_BRANCH]
    q, k, v, g = jnp.split(h @ w_in, cut, axis=-1)
    qc, kc, vc, gc = jnp.split(hc @ w_in, cut, axis=-1)
    q, k, v = to_heads(q, RET_HEADS), to_heads(k, RET_HEADS), to_heads(v, RET_HEADS)
    qc, kc, vc = to_heads(qc, RET_HEADS), to_heads(kc, RET_HEADS), to_heads(vc, RET_HEADS)
    q = rope_2d(q, row_pos, col_pos)
    k = rope_2d(k, row_pos, col_pos) * (RET_QK_DIM ** -0.5)
    kc = kc * (RET_QK_DIM ** -0.5)
    log_fwd = -jax.nn.softplus(raw_fwd.astype(jnp.float32))
    log_bwd = -jax.nn.softplus(raw_bwd.astype(jnp.float32))
    flip = lambda a: a[:, :, ::-1]
    zero = jnp.zeros((b, RET_HEADS, RET_QK_DIM, RET_V_DIM), h.dtype)
    oc_f, s_f = retention_scan(qc, kc, vc, log_fwd, zero)
    oc_b, s_b = retention_scan(flip(qc), flip(kc), flip(vc), log_bwd, zero)
    o_f, _ = retention_scan(q, k, v, log_fwd, s_f)
    o_b, _ = retention_scan(flip(q), flip(k), flip(v), log_bwd, s_b)
    o = head_norm(o_f + flip(o_b))
    y = (jax.nn.silu(g) * from_heads(o)) @ w_out
    if with_ctx_out:
        oc = head_norm(oc_f + flip(oc_b))
        yc = (jax.nn.silu(gc) * from_heads(oc)) @ w_out
        return y, yc
    return y, None


def na_branch(h, hc, w_in, rpb, w_out, with_ctx_out):
    b, t, _ = h.shape
    rows = t // GRID_W
    wh = min(NA_WIN_H, rows)
    scale = NA_HEAD_DIM ** -0.5
    q, k, v, g = jnp.split(h @ w_in, 4, axis=-1)
    qc, kc, vc, gc = jnp.split(hc @ w_in, 4, axis=-1)
    q, k, v = to_heads(q, NA_HEADS) * scale, to_heads(k, NA_HEADS), to_heads(v, NA_HEADS)
    qc, kc, vc = to_heads(qc, NA_HEADS) * scale, to_heads(kc, NA_HEADS), to_heads(vc, NA_HEADS)

    nqb = GRID_W // NA_QBLOCK_W
    kbw = NA_QBLOCK_W + NA_WIN_W
    qcol = np.arange(GRID_W).reshape(nqb, NA_QBLOCK_W)
    cb_start = np.clip(qcol[:, 0] - NA_WIN_W // 2, 0, GRID_W - kbw)
    kcol = cb_start[:, None] + np.arange(kbw)
    win_start = np.clip(qcol - NA_WIN_W // 2, 0, GRID_W - NA_WIN_W)
    col_valid = (kcol[:, None, :] >= win_start[..., None]) & (kcol[:, None, :] < win_start[..., None] + NA_WIN_W)
    col_idx = np.clip(kcol[:, None, :] - qcol[..., None] + NA_WIN_W - 1, 0, 2 * NA_WIN_W - 2)
    mask = jnp.asarray(np.broadcast_to(col_valid[:, :, None, :], (nqb, NA_QBLOCK_W, wh, kbw)).reshape(nqb, NA_QBLOCK_W, wh * kbw))

    q_grid = q.reshape(b, NA_HEADS, rows, GRID_W, NA_HEAD_DIM)
    k_grid = k.reshape(b, NA_HEADS, rows, GRID_W, NA_HEAD_DIM)
    v_grid = v.reshape(b, NA_HEADS, rows, GRID_W, NA_HEAD_DIM)

    def gather_block(a_grid, rs):
        a_rows = lax.dynamic_slice_in_dim(a_grid, rs, wh, axis=2)
        a_blk = a_rows[:, :, :, kcol]
        return a_blk.transpose(0, 1, 3, 2, 4, 5).reshape(b, NA_HEADS, nqb, wh * kbw, NA_HEAD_DIM)

    def row_fn(r):
        rs = jnp.clip(r - wh // 2, 0, rows - wh)
        k_blk = gather_block(k_grid, rs)
        v_blk = gather_block(v_grid, rs)
        q_r = lax.dynamic_index_in_dim(q_grid, r, axis=2, keepdims=False).reshape(b, NA_HEADS, nqb, NA_QBLOCK_W, NA_HEAD_DIM)
        row_idx = rs + jnp.arange(wh) - r + NA_WIN_H - 1
        bias = rpb[:, row_idx][:, :, col_idx]
        bias = bias.transpose(0, 2, 3, 1, 4).reshape(NA_HEADS, nqb, NA_QBLOCK_W, wh * kbw)
        s_loc = jnp.einsum('bhnqd,bhnkd->bhnqk', q_r, k_blk) + bias
        s_loc = jnp.where(mask, s_loc.astype(jnp.float32), -jnp.inf)
        s_ctx = jnp.einsum('bhnqd,bhcd->bhnqc', q_r, kc).astype(jnp.float32)
        p = jax.nn.softmax(jnp.concatenate([s_loc, s_ctx], axis=-1), axis=-1).astype(v.dtype)
        p_loc, p_ctx = p[..., :wh * kbw], p[..., wh * kbw:]
        o = jnp.einsum('bhnqk,bhnkd->bhnqd', p_loc, v_blk) + jnp.einsum('bhnqc,bhcd->bhnqd', p_ctx, vc)
        return o.reshape(b, NA_HEADS, GRID_W, NA_HEAD_DIM)

    o = lax.map(row_fn, jnp.arange(rows))
    o = o.transpose(1, 0, 3, 2, 4).reshape(b, t, NA_HEADS * NA_HEAD_DIM)
    y = (jax.nn.silu(g) * o) @ w_out
    if with_ctx_out:
        sc = jnp.einsum('bhqd,bhkd->bhqk', qc, kc).astype(jnp.float32)
        pc = jax.nn.softmax(sc, axis=-1).astype(vc.dtype)
        oc = from_heads(jnp.einsum('bhqk,bhkd->bhqd', pc, vc))
        yc = (jax.nn.silu(gc) * oc) @ w_out
        return y, yc
    return y, None


def setup_inputs(seed: int = 0) -> dict:
    key = jax.random.key(seed)
    ks = jax.random.split(key, 16)
    f32 = jnp.float32
    n_ret = (DEPTH + N_MIXERS - 1) // N_MIXERS
    n_na = DEPTH // N_MIXERS
    nrm = lambda k, s: jax.random.normal(k, s, f32)
    x = nrm(ks[0], (BATCH, SEQ, D_MODEL))
    c = nrm(ks[1], (BATCH, D_MODEL))
    ctx = nrm(ks[2], (BATCH, CTX_LEN, D_MODEL))
    c_ctx = nrm(ks[3], (D_MODEL,))
    mod_w = nrm(ks[4], (DEPTH, D_MODEL, 3 * D_MODEL)) * (0.5 * D_MODEL ** -0.5)
    mod_b = 0.01 * nrm(ks[5], (DEPTH, 3 * D_MODEL))
    norm_g = 1.0 + 0.02 * nrm(ks[6], (DEPTH, D_MODEL))
    ret_w_in = nrm(ks[7], (n_ret, D_MODEL, 2 * D_MODEL + 2 * D_BRANCH)) * (D_MODEL ** -0.5)
    neg_log_gamma = -jnp.log1p(-(2.0 ** (-5.0 - jnp.arange(RET_HEADS, dtype=f32))))
    raw = jnp.log(jnp.expm1(neg_log_gamma))
    ret_decay_fwd = raw[None] + 0.1 * nrm(ks[8], (n_ret, RET_HEADS))
    ret_decay_bwd = raw[None] + 0.1 * nrm(ks[9], (n_ret, RET_HEADS))
    ret_w_out = nrm(ks[10], (n_ret, D_BRANCH, D_MODEL)) * (D_BRANCH ** -0.5)
    na_w_in = nrm(ks[11], (n_na, D_MODEL, 4 * D_BRANCH)) * (D_MODEL ** -0.5)
    na_rpb = 0.02 * nrm(ks[12], (n_na, NA_HEADS, 2 * NA_WIN_H - 1, 2 * NA_WIN_W - 1))
    na_w_out = nrm(ks[13], (n_na, D_BRANCH, D_MODEL)) * (D_BRANCH ** -0.5)
    final_g = 1.0 + 0.02 * nrm(ks[14], (D_MODEL,))
    return {'x': x, 'c': c, 'ctx': ctx, 'c_ctx': c_ctx, 'mod_w': mod_w, 'mod_b': mod_b, 'norm_g': norm_g,
            'ret_w_in': ret_w_in, 'ret_decay_fwd': ret_decay_fwd, 'ret_decay_bwd': ret_decay_bwd, 'ret_w_out': ret_w_out,
            'na_w_in': na_w_in, 'na_rpb': na_rpb, 'na_w_out': na_w_out, 'final_g': final_g}


def reference(x, c, ctx, c_ctx, mod_w, mod_b, norm_g, ret_w_in, ret_decay_fwd, ret_decay_bwd, ret_w_out,
              na_w_in, na_rpb, na_w_out, final_g):
    n_tok = x.shape[1]
    t = jnp.arange(n_tok)
    row_pos = t // GRID_W
    col_pos = t % GRID_W
    cond_lat = jax.nn.silu(c)
    cond_ctx = jax.nn.silu(c_ctx)
    for l in range(DEPTH):
        last = l == DEPTH - 1
        sh, sc, gt = jnp.split(cond_lat @ mod_w[l] + mod_b[l], 3, axis=-1)
        shc, scc, gtc = jnp.split(cond_ctx @ mod_w[l] + mod_b[l], 3, axis=-1)
        h = rmsnorm(x, norm_g[l]) * (1.0 + sc[:, None]) + sh[:, None]
        hc = rmsnorm(ctx, norm_g[l]) * (1.0 + scc) + shc
        j = l // N_MIXERS
        if l % N_MIXERS == 0:
            y, yc = retention_branch(h, hc, ret_w_in[j], ret_decay_fwd[j], ret_decay_bwd[j], ret_w_out[j],
                                     row_pos, col_pos, not last)
        else:
            y, yc = na_branch(h, hc, na_w_in[j], na_rpb[j], na_w_out[j], not last)
        x = x + gt[:, None] * y
        if not last:
            ctx = ctx + gtc * yc
    return rmsnorm(x, final_g)
```

```python
import functools

import numpy as np
import jax
import jax.numpy as jnp
from jax import lax
from jax.experimental import pallas as pl
from jax.experimental.pallas import tpu as pltpu

D_MODEL = 2048
BATCH = 4
SEQ = 2048
DEPTH = 4
GRID_W = 64
GRID_H = SEQ // GRID_W
CTX_LEN = 256
TOK = CTX_LEN + SEQ
D_BRANCH = 2 * D_MODEL
RET_HEADS = 8
RET_QK_DIM = D_MODEL // RET_HEADS
RET_V_DIM = D_BRANCH // RET_HEADS
NA_HEADS = 32
NA_HEAD_DIM = D_BRANCH // NA_HEADS
NA_WIN_H = 8
NA_WIN_W = 16
ROPE_BASE = 10000.0
EPS = 1e-6

F32 = jnp.float32
BF16 = jnp.bfloat16

LANES = 128
VMEM_LIMIT = 56 * 1024 * 1024

RET_N = 2 * D_MODEL + 2 * D_BRANCH
NA_N = 4 * D_BRANCH

TM = TOK // 2
TN = 512
NORM_ROWS = 128
RET_CHUNK = 256
N_CHUNKS = TOK // RET_CHUNK

NA_QROWS = 4
NA_KROWS = 12
NA_QBLK = NA_QROWS * GRID_W
NA_KBLK = NA_KROWS * GRID_W
NA_NBLK = GRID_H // NA_QROWS
NEG_BIG = -1e30


def _dot(a, b):
    return jnp.dot(a, b, preferred_element_type=F32)


def _dot_nt(a, b):
    return lax.dot_general(a, b, (((1,), (1,)), ((), ())), preferred_element_type=F32)


def _dot_tn(a, b):
    return lax.dot_general(a, b, (((0,), (0,)), ((), ())), preferred_element_type=F32)


def _silu(x):
    return x * jax.nn.sigmoid(x)


def _is_ctx_rows(tile_idx, rows):
    tok = tile_idx * rows + lax.broadcasted_iota(jnp.int32, (rows, 1), 0)
    return tok < CTX_LEN


def _mod_kernel(c_ref, w_ref, b_ref, o_ref):
    cond = _silu(c_ref[...]).astype(BF16)
    o_ref[0] = _dot(cond, w_ref[0].astype(BF16)) + b_ref[0]


def _modulation(cvec, mod_w, mod_b):
    tn = 1024
    n = 3 * D_MODEL
    return pl.pallas_call(
        _mod_kernel,
        grid=(DEPTH, n // tn),
        in_specs=[
            pl.BlockSpec((8, D_MODEL), lambda l, j: (0, 0)),
            pl.BlockSpec((1, D_MODEL, tn), lambda l, j: (l, 0, j)),
            pl.BlockSpec((1, 1, tn), lambda l, j: (l, 0, j)),
        ],
        out_specs=pl.BlockSpec((1, 8, tn), lambda l, j: (l, 0, j)),
        out_shape=jax.ShapeDtypeStruct((DEPTH, 8, n), F32),
        compiler_params=pltpu.CompilerParams(
            dimension_semantics=("arbitrary", "arbitrary"), vmem_limit_bytes=VMEM_LIMIT),
        name="modulation",
    )(cvec, mod_w, mod_b.reshape(DEPTH, 1, n))


def _in_proj_kernel(x_ref, g_ref, sc_ref, sh_ref, w_ref, cos_ref, sin_ref, o_ref, h_scr, *, mode):
    mt = pl.program_id(1)
    j = pl.program_id(2)

    @pl.when(j == 0)
    def _():
        def slab(i, carry):
            r0 = pl.multiple_of(i * NORM_ROWS, NORM_ROWS)
            x = x_ref[0, pl.ds(r0, NORM_ROWS), :]
            ms = jnp.mean(x * x, axis=-1, keepdims=True)
            y = x * lax.rsqrt(ms + EPS) * g_ref[...]
            is_ctx = _is_ctx_rows(mt * (TM // NORM_ROWS) + i, NORM_ROWS)
            sc = jnp.where(is_ctx, sc_ref[0, 0:1, :], sc_ref[0, 1:2, :])
            sh = jnp.where(is_ctx, sh_ref[0, 0:1, :], sh_ref[0, 1:2, :])
            h_scr[pl.ds(r0, NORM_ROWS), :] = (y * (1.0 + sc) + sh).astype(BF16)
            return carry

        lax.fori_loop(0, TM // NORM_ROWS, slab, 0)

    acc = _dot(h_scr[...], w_ref[...])

    if mode == "ret":
        n_q = D_MODEL // TN
        n_rope = 2 * D_MODEL // TN

        @pl.when(j < n_rope)
        def _():
            scale = jnp.where(j < n_q, 1.0, RET_QK_DIM ** -0.5)
            for c0 in range(0, TN, LANES):
                t0 = c0 % RET_QK_DIM
                xg = acc[:, c0:c0 + LANES]
                r = xg * cos_ref[:, t0:t0 + LANES] + pltpu.roll(xg, LANES // 2, 1) * sin_ref[:, t0:t0 + LANES]
                o_ref[0, :, c0:c0 + LANES] = (r * scale).astype(BF16)

        @pl.when(j >= n_rope)
        def _():
            o_ref[0] = acc.astype(BF16)
    else:
        n_q = D_BRANCH // TN
        scale = jnp.where(j < n_q, NA_HEAD_DIM ** -0.5, 1.0)
        o_ref[0] = (acc * scale).astype(BF16)


def _in_proj(xs, g, sc, sh, w, cos_t, sin_t, mode):
    n = w.shape[1]
    return pl.pallas_call(
        functools.partial(_in_proj_kernel, mode=mode),
        grid=(BATCH, TOK // TM, n // TN),
        in_specs=[
            pl.BlockSpec((1, TM, D_MODEL), lambda b, m, j: (b, m, 0)),
            pl.BlockSpec((1, D_MODEL), lambda b, m, j: (0, 0)),
            pl.BlockSpec((1, 2, D_MODEL), lambda b, m, j: (b, 0, 0)),
            pl.BlockSpec((1, 2, D_MODEL), lambda b, m, j: (b, 0, 0)),
            pl.BlockSpec((D_MODEL, TN), lambda b, m, j: (0, j)),
            pl.BlockSpec((TM, RET_QK_DIM), lambda b, m, j: (m, 0)),
            pl.BlockSpec((TM, RET_QK_DIM), lambda b, m, j: (m, 0)),
        ],
        out_specs=pl.BlockSpec((1, TM, TN), lambda b, m, j: (b, m, j)),
        out_shape=jax.ShapeDtypeStruct((BATCH, TOK, n), BF16),
        scratch_shapes=[pltpu.VMEM((TM, D_MODEL), BF16)],
        compiler_params=pltpu.CompilerParams(
            dimension_semantics=("arbitrary", "arbitrary", "arbitrary"), vmem_limit_bytes=VMEM_LIMIT),
        name="in_proj_" + mode,
    )(xs, g, sc, sh, w, cos_t, sin_t)


def _out_proj_kernel(a_ref, w_ref, x_ref, gt_ref, o_ref):
    mt = pl.program_id(1)
    y = _dot(a_ref[0], w_ref[...])
    gt = jnp.where(_is_ctx_rows(mt, TM), gt_ref[0, 0:1, :], gt_ref[0, 1:2, :])
    o_ref[0] = x_ref[0] + gt * y


def _out_proj(a, w, xs, gt):
    return pl.pallas_call(
        _out_proj_kernel,
        grid=(BATCH, TOK // TM, D_MODEL // TN),
        in_specs=[
            pl.BlockSpec((1, TM, D_BRANCH), lambda b, m, j: (b, m, 0)),
            pl.BlockSpec((D_BRANCH, TN), lambda b, m, j: (0, j)),
            pl.BlockSpec((1, TM, TN), lambda b, m, j: (b, m, j)),
            pl.BlockSpec((1, 2, TN), lambda b, m, j: (b, 0, j)),
        ],
        out_specs=pl.BlockSpec((1, TM, TN), lambda b, m, j: (b, m, j)),
        out_shape=jax.ShapeDtypeStruct((BATCH, TOK, D_MODEL), F32),
        input_output_aliases={2: 0},
        compiler_params=pltpu.CompilerParams(
            dimension_semantics=("arbitrary", "arbitrary", "arbitrary"), vmem_limit_bytes=VMEM_LIMIT),
        name="out_proj",
    )(a, w, xs, gt)


def _ret_kernel(q_ref, k_ref, v_ref, g_ref, raw_ref, o_ref, s_scr, o_scr):
    c = RET_CHUNK
    raw = raw_ref[0]
    log_g = -(jnp.maximum(raw, 0.0) + jnp.log1p(jnp.exp(-jnp.abs(raw))))
    lg_f = log_g[0:1, 0:1]
    lg_b = log_g[1:2, 0:1]
    pos_i = lax.broadcasted_iota(jnp.int32, (c, 1), 0).astype(F32)
    ii = lax.broadcasted_iota(jnp.int32, (c, c), 0)
    jj = lax.broadcasted_iota(jnp.int32, (c, c), 1)
    diff = (ii - jj).astype(F32)

    def run(lg, reverse, order_fn, finalize):
        d = -diff if reverse else diff
        intra = jnp.where(d >= 0, jnp.exp(lg * jnp.maximum(d, 0.0)), 0.0)
        p = (c - 1.0 - pos_i) if reverse else pos_i
        q_dec = jnp.exp(lg * (p + 1.0))
        k_dec = jnp.exp(lg * (c - 1.0 - p))
        c_dec = jnp.exp(lg * float(c))
        s_scr[...] = jnp.zeros_like(s_scr)

        def step(t, carry):
            r0 = pl.multiple_of(order_fn(t) * c, c)
            qc = q_ref[0, pl.ds(r0, c), :]
            kc = k_ref[0, pl.ds(r0, c), :]
            vc = v_ref[0, pl.ds(r0, c), :]
            s = s_scr[...]
            scores = _dot_nt(qc, kc) * intra
            o = _dot(scores.astype(BF16), vc) + _dot(qc, s.astype(BF16)) * q_dec
            k_scaled = (kc.astype(F32) * k_dec).astype(BF16)
            s_scr[...] = s * c_dec + _dot_tn(k_scaled, vc)
            if not finalize:
                o_scr[pl.ds(r0, c), :] = o
            else:
                tot = o_scr[pl.ds(r0, c), :] + o
                nrm = tot * lax.rsqrt(jnp.mean(tot * tot, axis=-1, keepdims=True) + EPS)
                gate = _silu(g_ref[0, pl.ds(r0, c), :].astype(F32))
                o_ref[0, pl.ds(r0, c), :] = (gate * nrm).astype(BF16)
            return carry

        lax.fori_loop(0, N_CHUNKS, step, 0)

    run(lg_f, False, lambda t: t, False)
    run(lg_b, True, lambda t: jnp.where(t == 0, 0, N_CHUNKS - t), True)


def _ret_core(qkvg, raw):
    nq = D_MODEL // RET_QK_DIM
    nv = 2 * D_MODEL // RET_V_DIM
    ng = nv + RET_HEADS
    return pl.pallas_call(
        _ret_kernel,
        grid=(BATCH, RET_HEADS),
        in_specs=[
            pl.BlockSpec((1, TOK, RET_QK_DIM), lambda b, h: (b, 0, h)),
            pl.BlockSpec((1, TOK, RET_QK_DIM), lambda b, h: (b, 0, nq + h)),
            pl.BlockSpec((1, TOK, RET_V_DIM), lambda b, h: (b, 0, nv + h)),
            pl.BlockSpec((1, TOK, RET_V_DIM), lambda b, h: (b, 0, ng + h)),
            pl.BlockSpec((1, 2, LANES), lambda b, h: (h, 0, 0)),
        ],
        out_specs=pl.BlockSpec((1, TOK, RET_V_DIM), lambda b, h: (b, 0, h)),
        out_shape=jax.ShapeDtypeStruct((BATCH, TOK, D_BRANCH), BF16),
        scratch_shapes=[pltpu.VMEM((RET_QK_DIM, RET_V_DIM), F32), pltpu.VMEM((TOK, RET_V_DIM), F32)],
        compiler_params=pltpu.CompilerParams(
            dimension_semantics=("arbitrary", "arbitrary"), vmem_limit_bytes=VMEM_LIMIT),
        name="ret_core",
    )(qkvg, qkvg, qkvg, qkvg, raw)


def _na_block_geometry():
    a = np.arange(NA_QROWS)[:, None, None, None]
    cq = np.arange(GRID_W)[None, :, None, None]
    jk = np.arange(NA_KROWS)[None, None, :, None]
    ck = np.arange(GRID_W)[None, None, None, :]

    def kind(i):
        ks = int(np.clip(NA_QROWS * i - NA_WIN_H // 2, 0, GRID_H - NA_KROWS))
        r = NA_QROWS * i + a
        rs = np.clip(r - NA_WIN_H // 2, 0, GRID_H - NA_WIN_H)
        kr = ks + jk
        ws = np.clip(cq - NA_WIN_W // 2, 0, GRID_W - NA_WIN_W)
        valid = (kr >= rs) & (kr < rs + NA_WIN_H) & (ck >= ws) & (ck < ws + NA_WIN_W)
        dr = np.clip(kr - r + NA_WIN_H - 1, 0, 2 * NA_WIN_H - 2)
        dc = np.clip(ck - cq + NA_WIN_W - 1, 0, 2 * NA_WIN_W - 2)
        shape = (NA_QROWS, GRID_W, NA_KROWS, GRID_W)
        flat = lambda z: np.broadcast_to(z, shape).reshape(NA_QBLK, NA_KBLK)
        return flat(valid), flat(dr), flat(dc)

    kinds = [kind(0), kind(1), kind(NA_NBLK - 1)]
    for i in range(2, NA_NBLK - 1):
        for u, w in zip(kind(i), kinds[1]):
            assert (u == w).all()
    return tuple(np.stack(z) for z in zip(*kinds))


def _na_key_start(i):
    return int(np.clip(NA_QROWS * i - NA_WIN_H // 2, 0, GRID_H - NA_KROWS))


def _na_kernel(q_ref, k_ref, v_ref, g_ref, bias_ref, o_ref):
    kc = k_ref[0, 0:CTX_LEN, :]
    vc = v_ref[0, 0:CTX_LEN, :]

    def finish(o, l, r0, rows):
        gate = _silu(g_ref[0, r0:r0 + rows, :].astype(F32))
        o_ref[0, r0:r0 + rows, :] = (gate * (o / l)).astype(BF16)

    s = _dot_nt(q_ref[0, 0:CTX_LEN, :], kc)
    p = jnp.exp(s - jnp.max(s, axis=-1, keepdims=True))
    finish(_dot(p.astype(BF16), vc), jnp.sum(p, axis=-1, keepdims=True), 0, CTX_LEN)

    for i in range(NA_NBLK):
        kind = 0 if i == 0 else (2 if i == NA_NBLK - 1 else 1)
        r0 = CTX_LEN + i * NA_QBLK
        k0 = CTX_LEN + _na_key_start(i) * GRID_W
        q = q_ref[0, r0:r0 + NA_QBLK, :]
        s_loc = _dot_nt(q, k_ref[0, k0:k0 + NA_KBLK, :]) + bias_ref[0, kind]
        s_ctx = _dot_nt(q, kc)
        m = jnp.maximum(jnp.max(s_loc, axis=-1, keepdims=True), jnp.max(s_ctx, axis=-1, keepdims=True))
        p_loc = jnp.exp(s_loc - m)
        p_ctx = jnp.exp(s_ctx - m)
        l = jnp.sum(p_loc, axis=-1, keepdims=True) + jnp.sum(p_ctx, axis=-1, keepdims=True)
        o = _dot(p_loc.astype(BF16), v_ref[0, k0:k0 + NA_KBLK, :]) + _dot(p_ctx.astype(BF16), vc)
        finish(o, l, r0, NA_QBLK)


def _na_core(qkvg, bias):
    nh = NA_HEADS
    return pl.pallas_call(
        _na_kernel,
        grid=(NA_HEADS, BATCH),
        in_specs=[
            pl.BlockSpec((1, TOK, NA_HEAD_DIM), lambda h, b: (b, 0, h)),
            pl.BlockSpec((1, TOK, NA_HEAD_DIM), lambda h, b: (b, 0, nh + h)),
            pl.BlockSpec((1, TOK, NA_HEAD_DIM), lambda h, b: (b, 0, 2 * nh + h)),
            pl.BlockSpec((1, TOK, NA_HEAD_DIM), lambda h, b: (b, 0, 3 * nh + h)),
            pl.BlockSpec((1, 3, NA_QBLK, NA_KBLK), lambda h, b: (h, 0, 0, 0)),
        ],
        out_specs=pl.BlockSpec((1, TOK, NA_HEAD_DIM), lambda h, b: (b, 0, h)),
        out_shape=jax.ShapeDtypeStruct((BATCH, TOK, D_BRANCH), BF16),
        compiler_params=pltpu.CompilerParams(
            dimension_semantics=("arbitrary", "arbitrary"), vmem_limit_bytes=VMEM_LIMIT),
        name="na_core",
    )(qkvg, qkvg, qkvg, qkvg, bias)


def _final_norm_kernel(x_ref, g_ref, o_ref):
    x = x_ref[0]
    o_ref[0] = x * lax.rsqrt(jnp.mean(x * x, axis=-1, keepdims=True) + EPS) * g_ref[...]


def _final_norm(xs, g):
    tm = CTX_LEN
    return pl.pallas_call(
        _final_norm_kernel,
        grid=(BATCH, SEQ // tm),
        in_specs=[
            pl.BlockSpec((1, tm, D_MODEL), lambda b, i: (b, i + CTX_LEN // tm, 0)),
            pl.BlockSpec((1, D_MODEL), lambda b, i: (0, 0)),
        ],
        out_specs=pl.BlockSpec((1, tm, D_MODEL), lambda b, i: (b, i, 0)),
        out_shape=jax.ShapeDtypeStruct((BATCH, SEQ, D_MODEL), F32),
        compiler_params=pltpu.CompilerParams(
            dimension_semantics=("arbitrary", "arbitrary"), vmem_limit_bytes=VMEM_LIMIT),
        name="final_norm",
    )(xs, g)


def _rope_tables():
    quarter = RET_QK_DIM // 4
    freqs = ROPE_BASE ** (-jnp.arange(quarter, dtype=F32) / quarter)
    t = jnp.arange(SEQ)
    ang_r = (t // GRID_W).astype(F32)[:, None] * freqs[None, :]
    ang_c = (t % GRID_W).astype(F32)[:, None] * freqs[None, :]
    cos = jnp.concatenate([jnp.cos(ang_r)] * 2 + [jnp.cos(ang_c)] * 2, axis=-1)
    sin = jnp.concatenate([-jnp.sin(ang_r), jnp.sin(ang_r), -jnp.sin(ang_c), jnp.sin(ang_c)], axis=-1)
    cos = jnp.concatenate([jnp.ones((CTX_LEN, RET_QK_DIM), F32), cos], axis=0)
    sin = jnp.concatenate([jnp.zeros((CTX_LEN, RET_QK_DIM), F32), sin], axis=0)
    return cos, sin


def _per_batch(m):
    ctx_row = jnp.broadcast_to(m[BATCH][None, None, :], (BATCH, 1, m.shape[-1]))
    return jnp.concatenate([ctx_row, m[:BATCH][:, None, :]], axis=1)


def kernel(x, c, ctx, c_ctx, mod_w, mod_b, norm_g, ret_w_in, ret_decay_fwd, ret_decay_bwd, ret_w_out,
           na_w_in, na_rpb, na_w_out, final_g):
    xs = jnp.concatenate([ctx, x], axis=1)
    cvec = jnp.concatenate([c, c_ctx[None], jnp.zeros((8 - BATCH - 1, D_MODEL), F32)], axis=0)
    mods = _modulation(cvec, mod_w, mod_b)
    cos_t, sin_t = _rope_tables()
    valid, dr, dc = _na_block_geometry()

    for l in range(DEPTH):
        j = l // 2
        sh = _per_batch(mods[l, :, :D_MODEL])
        sc = _per_batch(mods[l, :, D_MODEL:2 * D_MODEL])
        gt = _per_batch(mods[l, :, 2 * D_MODEL:])
        g = norm_g[l][None, :]
        if l % 2 == 0:
            qkvg = _in_proj(xs, g, sc, sh, ret_w_in[j].astype(BF16), cos_t, sin_t, "ret")
            raw = jnp.stack([ret_decay_fwd[j], ret_decay_bwd[j]], axis=1)
            raw = jnp.broadcast_to(raw[:, :, None], (RET_HEADS, 2, LANES))
            a = _ret_core(qkvg, raw)
            w_out = ret_w_out[j]
        else:
            qkvg = _in_proj(xs, g, sc, sh, na_w_in[j].astype(BF16), cos_t, sin_t, "na")
            bias = jnp.where(valid[None], na_rpb[j][:, dr, dc], NEG_BIG)
            a = _na_core(qkvg, bias)
            w_out = na_w_out[j]
        xs = _out_proj(a, w_out.astype(BF16), xs, gt)
    return _final_norm(xs, final_g[None, :])
```

```python
import functools

import numpy as np
import jax
import jax.numpy as jnp
from jax import lax
from jax.experimental import pallas as pl
from jax.experimental.pallas import tpu as pltpu

D_MODEL = 2048
BATCH = 4
SEQ = 2048
DEPTH = 4
GRID_W = 64
GRID_H = SEQ // GRID_W
CTX_LEN = 256
TOK = CTX_LEN + SEQ
D_BRANCH = 2 * D_MODEL
RET_HEADS = 8
RET_QK_DIM = D_MODEL // RET_HEADS
RET_V_DIM = D_BRANCH // RET_HEADS
NA_HEADS = 32
NA_HEAD_DIM = D_BRANCH // NA_HEADS
NA_WIN_H = 8
NA_WIN_W = 16
ROPE_BASE = 10000.0
EPS = 1e-6

F32 = jnp.float32
BF16 = jnp.bfloat16

LANES = 128
VMEM_LIMIT = 56 * 1024 * 1024

RET_N = 2 * D_MODEL + 2 * D_BRANCH
NA_N = 4 * D_BRANCH

TM = TOK // 2
TN = 512
NORM_ROWS = 128
RET_CHUNK = 256
N_CHUNKS = TOK // RET_CHUNK

NA_QROWS = 4
NA_KROWS = 12
NA_QBLK = NA_QROWS * GRID_W
NA_KBLK = NA_KROWS * GRID_W
NA_NBLK = GRID_H // NA_QROWS
NEG_BIG = -1e30


def _dot(a, b):
    return jnp.dot(a, b, preferred_element_type=F32)


def _dot_nt(a, b):
    return lax.dot_general(a, b, (((1,), (1,)), ((), ())), preferred_element_type=F32)


def _dot_tn(a, b):
    return lax.dot_general(a, b, (((0,), (0,)), ((), ())), preferred_element_type=F32)


def _silu(x):
    return x * jax.nn.sigmoid(x)


def _is_ctx_rows(tile_idx, rows):
    tok = tile_idx * rows + lax.broadcasted_iota(jnp.int32, (rows, 1), 0)
    return tok < CTX_LEN


def _mod_kernel(c_ref, w_ref, b_ref, o_ref):
    cond = _silu(c_ref[...]).astype(BF16)
    o_ref[0] = _dot(cond, w_ref[0].astype(BF16)) + b_ref[0]


def _modulation(cvec, mod_w, mod_b):
    tn = 1024
    n = 3 * D_MODEL
    return pl.pallas_call(
        _mod_kernel,
        grid=(DEPTH, n // tn),
        in_specs=[
            pl.BlockSpec((8, D_MODEL), lambda l, j: (0, 0)),
            pl.BlockSpec((1, D_MODEL, tn), lambda l, j: (l, 0, j)),
            pl.BlockSpec((1, 1, tn), lambda l, j: (l, 0, j)),
        ],
        out_specs=pl.BlockSpec((1, 8, tn), lambda l, j: (l, 0, j)),
        out_shape=jax.ShapeDtypeStruct((DEPTH, 8, n), F32),
        compiler_params=pltpu.CompilerParams(
            dimension_semantics=("arbitrary", "arbitrary"), vmem_limit_bytes=VMEM_LIMIT),
        name="modulation",
    )(cvec, mod_w, mod_b.reshape(DEPTH, 1, n))


def _in_proj_kernel(x_ref, g_ref, sc_ref, sh_ref, w_ref, cos_ref, sin_ref, o_ref, h_scr, *, mode):
    mt = pl.program_id(1)
    j = pl.program_id(2)

    @pl.when(j == 0)
    def _():
        def slab(i, carry):
            r0 = pl.multiple_of(i * NORM_ROWS, NORM_ROWS)
            x = x_ref[0, pl.ds(r0, NORM_ROWS), :]
            ms = jnp.mean(x * x, axis=-1, keepdims=True)
            y = x * lax.rsqrt(ms + EPS) * g_ref[...]
            is_ctx = _is_ctx_rows(mt * (TM // NORM_ROWS) + i, NORM_ROWS)
            sc = jnp.where(is_ctx, sc_ref[0, 0:1, :], sc_ref[0, 1:2, :])
            sh = jnp.where(is_ctx, sh_ref[0, 0:1, :], sh_ref[0, 1:2, :])
            h_scr[pl.ds(r0, NORM_ROWS), :] = (y * (1.0 + sc) + sh).astype(BF16)
            return carry

        lax.fori_loop(0, TM // NORM_ROWS, slab, 0)

    acc = _dot(h_scr[...], w_ref[...])

    if mode == "ret":
        n_q = D_MODEL // TN
        n_rope = 2 * D_MODEL // TN

        @pl.when(j < n_rope)
        def _():
            scale = jnp.where(j < n_q, 1.0, RET_QK_DIM ** -0.5)
            for c0 in range(0, TN, LANES):
                t0 = c0 % RET_QK_DIM
                xg = acc[:, c0:c0 + LANES]
                r = xg * cos_ref[:, t0:t0 + LANES] + pltpu.roll(xg, LANES // 2, 1) * sin_ref[:, t0:t0 + LANES]
                o_ref[0, :, c0:c0 + LANES] = (r * scale).astype(BF16)

        @pl.when(j >= n_rope)
        def _():
            o_ref[0] = acc.astype(BF16)
    else:
        n_q = D_BRANCH // TN
        scale = jnp.where(j < n_q, NA_HEAD_DIM ** -0.5, 1.0)
        o_ref[0] = (acc * scale).astype(BF16)


def _in_proj(xs, g, sc, sh, w, cos_t, sin_t, mode):
    n = w.shape[1]
    return pl.pallas_call(
        functools.partial(_in_proj_kernel, mode=mode),
        grid=(BATCH, TOK // TM, n // TN),
        in_specs=[
            pl.BlockSpec((1, TM, D_MODEL), lambda b, m, j: (b, m, 0)),
            pl.BlockSpec((1, D_MODEL), lambda b, m, j: (0, 0)),
            pl.BlockSpec((1, 2, D_MODEL), lambda b, m, j: (b, 0, 0)),
            pl.BlockSpec((1, 2, D_MODEL), lambda b, m, j: (b, 0, 0)),
            pl.BlockSpec((D_MODEL, TN), lambda b, m, j: (0, j)),
            pl.BlockSpec((TM, RET_QK_DIM), lambda b, m, j: (m, 0)),
            pl.BlockSpec((TM, RET_QK_DIM), lambda b, m, j: (m, 0)),
        ],
        out_specs=pl.BlockSpec((1, TM, TN), lambda b, m, j: (b, m, j)),
        out_shape=jax.ShapeDtypeStruct((BATCH, TOK, n), BF16),
        scratch_shapes=[pltpu.VMEM((TM, D_MODEL), BF16)],
        compiler_params=pltpu.CompilerParams(
            dimension_semantics=("arbitrary", "arbitrary", "arbitrary"), vmem_limit_bytes=VMEM_LIMIT),
        name="in_proj_" + mode,
    )(xs, g, sc, sh, w, cos_t, sin_t)


def _out_proj_kernel(a_ref, w_ref, x_ref, gt_ref, o_ref):
    mt = pl.program_id(1)
    y = _dot(a_ref[0], w_ref[...])
    gt = jnp.where(_is_ctx_rows(mt, TM), gt_ref[0, 0:1, :], gt_ref[0, 1:2, :])
    o_ref[0] = x_ref[0] + gt * y


def _out_proj(a, w, xs, gt):
    return pl.pallas_call(
        _out_proj_kernel,
        grid=(BATCH, TOK // TM, D_MODEL // TN),
        in_specs=[
            pl.BlockSpec((1, TM, D_BRANCH), lambda b, m, j: (b, m, 0)),
            pl.BlockSpec((D_BRANCH, TN), lambda b, m, j: (0, j)),
            pl.BlockSpec((1, TM, TN), lambda b, m, j: (b, m, j)),
            pl.BlockSpec((1, 2, TN), lambda b, m, j: (b, 0, j)),
        ],
        out_specs=pl.BlockSpec((1, TM, TN), lambda b, m, j: (b, m, j)),
        out_shape=jax.ShapeDtypeStruct((BATCH, TOK, D_MODEL), F32),
        input_output_aliases={2: 0},
        compiler_params=pltpu.CompilerParams(
            dimension_semantics=("arbitrary", "arbitrary", "arbitrary"), vmem_limit_bytes=VMEM_LIMIT),
        name="out_proj",
    )(a, w, xs, gt)


def _ret_kernel(q_ref, k_ref, v_ref, g_ref, raw_ref, o_ref, s_scr, o_scr):
    c = RET_CHUNK
    raw = raw_ref[0]
    log_g = -(jnp.maximum(raw, 0.0) + jnp.log1p(jnp.exp(-jnp.abs(raw))))
    lg_f = log_g[0:1, 0:1]
    lg_b = log_g[1:2, 0:1]
    pos_i = lax.broadcasted_iota(jnp.int32, (c, 1), 0).astype(F32)
    ii = lax.broadcasted_iota(jnp.int32, (c, c), 0)
    jj = lax.broadcasted_iota(jnp.int32, (c, c), 1)
    diff = (ii - jj).astype(F32)

    def run(lg, reverse, order_fn, finalize):
        d = -diff if reverse else diff
        intra = jnp.where(d >= 0, jnp.exp(lg * jnp.maximum(d, 0.0)), 0.0)
        p = (c - 1.0 - pos_i) if reverse else pos_i
        q_dec = jnp.exp(lg * (p + 1.0))
        k_dec = jnp.exp(lg * (c - 1.0 - p))
        c_dec = jnp.exp(lg * float(c))
        s_scr[...] = jnp.zeros_like(s_scr)

        def step(t, carry):
            r0 = pl.multiple_of(order_fn(t) * c, c)
            qc = q_ref[0, pl.ds(r0, c), :]
            kc = k_ref[0, pl.ds(r0, c), :]
            vc = v_ref[0, pl.ds(r0, c), :]
            s = s_scr[...]
            scores = _dot_nt(qc, kc) * intra
            o = _dot(scores.astype(BF16), vc) + _dot(qc, s.astype(BF16)) * q_dec
            k_scaled = (kc.astype(F32) * k_dec).astype(BF16)
            s_scr[...] = s * c_dec + _dot_tn(k_scaled, vc)
            if not finalize:
                o_scr[pl.ds(r0, c), :] = o
            else:
                tot = o_scr[pl.ds(r0, c), :] + o
                nrm = tot * lax.rsqrt(jnp.mean(tot * tot, axis=-1, keepdims=True) + EPS)
                gate = _silu(g_ref[0, pl.ds(r0, c), :].astype(F32))
                o_ref[0, pl.ds(r0, c), :] = (gate * nrm).astype(BF16)
            return carry

        lax.fori_loop(0, N_CHUNKS, step, 0)

    run(lg_f, False, lambda t: t, False)
    run(lg_b, True, lambda t: jnp.where(t == 0, 0, N_CHUNKS - t), True)


def _ret_core(qkvg, raw):
    nq = D_MODEL // RET_QK_DIM
    nv = 2 * D_MODEL // RET_V_DIM
    ng = nv + RET_HEADS
    return pl.pallas_call(
        _ret_kernel,
        grid=(BATCH, RET_HEADS),
        in_specs=[
            pl.BlockSpec((1, TOK, RET_QK_DIM), lambda b, h: (b, 0, h)),
            pl.BlockSpec((1, TOK, RET_QK_DIM), lambda b, h: (b, 0, nq + h)),
            pl.BlockSpec((1, TOK, RET_V_DIM), lambda b, h: (b, 0, nv + h)),
            pl.BlockSpec((1, TOK, RET_V_DIM), lambda b, h: (b, 0, ng + h)),
            pl.BlockSpec((1, 2, LANES), lambda b, h: (h, 0, 0)),
        ],
        out_specs=pl.BlockSpec((1, TOK, RET_V_DIM), lambda b, h: (b, 0, h)),
        out_shape=jax.ShapeDtypeStruct((BATCH, TOK, D_BRANCH), BF16),
        scratch_shapes=[pltpu.VMEM((RET_QK_DIM, RET_V_DIM), F32), pltpu.VMEM((TOK, RET_V_DIM), F32)],
        compiler_params=pltpu.CompilerParams(
            dimension_semantics=("arbitrary", "arbitrary"), vmem_limit_bytes=VMEM_LIMIT),
        name="ret_core",
    )(qkvg, qkvg, qkvg, qkvg, raw)


def _na_key_start(i):
    return int(np.clip(NA_QROWS * i - NA_WIN_H // 2, 0, GRID_H - NA_KROWS))


def _na_row_geometry(i, a, j):
    r = NA_QROWS * i + a
    rs = int(np.clip(r - NA_WIN_H // 2, 0, GRID_H - NA_WIN_H))
    kr = _na_key_start(i) + j
    return rs <= kr < rs + NA_WIN_H, kr - r + NA_WIN_H - 1


NA_KIND_BLOCKS = (0, 1, NA_NBLK - 1)
for _i in range(2, NA_NBLK - 1):
    assert all(_na_row_geometry(_i, _a, _j) == _na_row_geometry(1, _a, _j)
               for _a in range(NA_QROWS) for _j in range(NA_KROWS))


def _rpb_pair_table(rpb):
    mid = NA_WIN_W - 1
    pad = jnp.zeros(rpb.shape[:2] + (LANES - rpb.shape[2],), rpb.dtype)
    centred = jnp.concatenate([rpb[..., mid:], pad, rpb[..., :mid]], axis=-1)
    shifted = jnp.roll(centred, GRID_W, axis=-1)
    lane = jnp.arange(LANES)
    around_64 = (lane >= GRID_W // 2) & (lane < LANES - GRID_W // 2)
    pair = jnp.where(around_64, shifted[:, 1:], centred[:, :-1])
    return jnp.concatenate([pair, jnp.zeros((rpb.shape[0], 2, LANES), rpb.dtype)], axis=1)


def _na_build_bias(pair_ref, bias_scr):
    c = lax.broadcasted_iota(jnp.int32, (GRID_W, LANES), 0)
    lane = lax.broadcasted_iota(jnp.int32, (GRID_W, LANES), 1)
    kc = lane & (GRID_W - 1)
    ws = jnp.clip(c - NA_WIN_W // 2, 0, GRID_W - NA_WIN_W)
    in_cols = (kc >= ws) & (kc < ws + NA_WIN_W)
    first = lane < GRID_W
    masks = {(True, True): in_cols,
             (True, False): in_cols & first,
             (False, True): in_cols & jnp.logical_not(first)}
    toeplitz = {}
    for kind, i in enumerate(NA_KIND_BLOCKS):
        for a in range(NA_QROWS):
            for grp in range(NA_KROWS // 2):
                v0, d0 = _na_row_geometry(i, a, 2 * grp)
                v1, _ = _na_row_geometry(i, a, 2 * grp + 1)
                if v0 or v1:
                    assert 0 <= d0 <= 2 * NA_WIN_H - 3
                    if d0 not in toeplitz:
                        base = jnp.broadcast_to(pair_ref[0, d0:d0 + 1, :], (GRID_W, LANES))
                        toeplitz[d0] = pltpu.roll(base, 0, 1, stride=1, stride_axis=0)
                    tile = jnp.where(masks[(v0, v1)], toeplitz[d0], NEG_BIG)
                else:
                    tile = jnp.full((GRID_W, LANES), NEG_BIG, F32)
                bias_scr[kind, a * GRID_W:(a + 1) * GRID_W, grp * LANES:(grp + 1) * LANES] = tile


def _na_kernel(q_ref, k_ref, v_ref, g_ref, pair_ref, o_ref, bias_ref):
    @pl.when(pl.program_id(1) == 0)
    def _():
        _na_build_bias(pair_ref, bias_ref)

    kc = k_ref[0, 0:CTX_LEN, :]
    vc = v_ref[0, 0:CTX_LEN, :]

    def finish(o, l, r0, rows):
        gate = _silu(g_ref[0, r0:r0 + rows, :].astype(F32))
        o_ref[0, r0:r0 + rows, :] = (gate * (o / l)).astype(BF16)

    s = _dot_nt(q_ref[0, 0:CTX_LEN, :], kc)
    p = jnp.exp(s - jnp.max(s, axis=-1, keepdims=True))
    finish(_dot(p.astype(BF16), vc), jnp.sum(p, axis=-1, keepdims=True), 0, CTX_LEN)

    for i in range(NA_NBLK):
        kind = 0 if i == 0 else (2 if i == NA_NBLK - 1 else 1)
        r0 = CTX_LEN + i * NA_QBLK
        k0 = CTX_LEN + _na_key_start(i) * GRID_W
        q = q_ref[0, r0:r0 + NA_QBLK, :]
        s_loc = _dot_nt(q, k_ref[0, k0:k0 + NA_KBLK, :]) + bias_ref[kind]
        s_ctx = _dot_nt(q, kc)
        m = jnp.maximum(jnp.max(s_loc, axis=-1, keepdims=True), jnp.max(s_ctx, axis=-1, keepdims=True))
        p_loc = jnp.exp(s_loc - m)
        p_ctx = jnp.exp(s_ctx - m)
        l = jnp.sum(p_loc, axis=-1, keepdims=True) + jnp.sum(p_ctx, axis=-1, keepdims=True)
        o = _dot(p_loc.astype(BF16), v_ref[0, k0:k0 + NA_KBLK, :]) + _dot(p_ctx.astype(BF16), vc)
        finish(o, l, r0, NA_QBLK)


def _na_core(qkvg, rpb_pairs):
    nh = NA_HEADS
    return pl.pallas_call(
        _na_kernel,
        grid=(NA_HEADS, BATCH),
        in_specs=[
            pl.BlockSpec((1, TOK, NA_HEAD_DIM), lambda h, b: (b, 0, h)),
            pl.BlockSpec((1, TOK, NA_HEAD_DIM), lambda h, b: (b, 0, nh + h)),
            pl.BlockSpec((1, TOK, NA_HEAD_DIM), lambda h, b: (b, 0, 2 * nh + h)),
            pl.BlockSpec((1, TOK, NA_HEAD_DIM), lambda h, b: (b, 0, 3 * nh + h)),
            pl.BlockSpec((1, 2 * NA_WIN_H, LANES), lambda h, b: (h, 0, 0)),
        ],
        out_specs=pl.BlockSpec((1, TOK, NA_HEAD_DIM), lambda h, b: (b, 0, h)),
        out_shape=jax.ShapeDtypeStruct((BATCH, TOK, D_BRANCH), BF16),
        scratch_shapes=[pltpu.VMEM((len(NA_KIND_BLOCKS), NA_QBLK, NA_KBLK), F32)],
        compiler_params=pltpu.CompilerParams(
            dimension_semantics=("arbitrary", "arbitrary"), vmem_limit_bytes=VMEM_LIMIT),
        name="na_core",
    )(qkvg, qkvg, qkvg, qkvg, rpb_pairs)


def _final_norm_kernel(x_ref, g_ref, o_ref):
    x = x_ref[0]
    o_ref[0] = x * lax.rsqrt(jnp.mean(x * x, axis=-1, keepdims=True) + EPS) * g_ref[...]


def _final_norm(xs, g):
    tm = CTX_LEN
    return pl.pallas_call(
        _final_norm_kernel,
        grid=(BATCH, SEQ // tm),
        in_specs=[
            pl.BlockSpec((1, tm, D_MODEL), lambda b, i: (b, i + CTX_LEN // tm, 0)),
            pl.BlockSpec((1, D_MODEL), lambda b, i: (0, 0)),
        ],
        out_specs=pl.BlockSpec((1, tm, D_MODEL), lambda b, i: (b, i, 0)),
        out_shape=jax.ShapeDtypeStruct((BATCH, SEQ, D_MODEL), F32),
        compiler_params=pltpu.CompilerParams(
            dimension_semantics=("arbitrary", "arbitrary"), vmem_limit_bytes=VMEM_LIMIT),
        name="final_norm",
    )(xs, g)


def _rope_tables():
    quarter = RET_QK_DIM // 4
    freqs = ROPE_BASE ** (-jnp.arange(quarter, dtype=F32) / quarter)
    t = jnp.arange(SEQ)
    ang_r = (t // GRID_W).astype(F32)[:, None] * freqs[None, :]
    ang_c = (t % GRID_W).astype(F32)[:, None] * freqs[None, :]
    cos = jnp.concatenate([jnp.cos(ang_r)] * 2 + [jnp.cos(ang_c)] * 2, axis=-1)
    sin = jnp.concatenate([-jnp.sin(ang_r), jnp.sin(ang_r), -jnp.sin(ang_c), jnp.sin(ang_c)], axis=-1)
    cos = jnp.concatenate([jnp.ones((CTX_LEN, RET_QK_DIM), F32), cos], axis=0)
    sin = jnp.concatenate([jnp.zeros((CTX_LEN, RET_QK_DIM), F32), sin], axis=0)
    return cos, sin


def _per_batch(m):
    ctx_row = jnp.broadcast_to(m[BATCH][None, None, :], (BATCH, 1, m.shape[-1]))
    return jnp.concatenate([ctx_row, m[:BATCH][:, None, :]], axis=1)


def kernel(x, c, ctx, c_ctx, mod_w, mod_b, norm_g, ret_w_in, ret_decay_fwd, ret_decay_bwd, ret_w_out,
           na_w_in, na_rpb, na_w_out, final_g):
    xs = jnp.concatenate([ctx, x], axis=1)
    cvec = jnp.concatenate([c, c_ctx[None], jnp.zeros((8 - BATCH - 1, D_MODEL), F32)], axis=0)
    mods = _modulation(cvec, mod_w, mod_b)
    cos_t, sin_t = _rope_tables()

    for l in range(DEPTH):
        j = l // 2
        sh = _per_batch(mods[l, :, :D_MODEL])
        sc = _per_batch(mods[l, :, D_MODEL:2 * D_MODEL])
        gt = _per_batch(mods[l, :, 2 * D_MODEL:])
        g = norm_g[l][None, :]
        if l % 2 == 0:
            qkvg = _in_proj(xs, g, sc, sh, ret_w_in[j].astype(BF16), cos_t, sin_t, "ret")
            raw = jnp.stack([ret_decay_fwd[j], ret_decay_bwd[j]], axis=1)
            raw = jnp.broadcast_to(raw[:, :, None], (RET_HEADS, 2, LANES))
            a = _ret_core(qkvg, raw)
            w_out = ret_w_out[j]
        else:
            qkvg = _in_proj(xs, g, sc, sh, na_w_in[j].astype(BF16), cos_t, sin_t, "na")
            a = _na_core(qkvg, _rpb_pair_table(na_rpb[j]))
            w_out = na_w_out[j]
        xs = _out_proj(a, w_out.astype(BF16), xs, gt)
    return _final_norm(xs, final_g[None, :])
```

```python
import functools

import numpy as np
import jax
import jax.numpy as jnp
from jax import lax
from jax.experimental import pallas as pl
from jax.experimental.pallas import tpu as pltpu

D_MODEL = 2048
BATCH = 4
SEQ = 2048
DEPTH = 4
GRID_W = 64
GRID_H = SEQ // GRID_W
CTX_LEN = 256
TOK = CTX_LEN + SEQ
D_BRANCH = 2 * D_MODEL
RET_HEADS = 8
RET_QK_DIM = D_MODEL // RET_HEADS
RET_V_DIM = D_BRANCH // RET_HEADS
NA_HEADS = 32
NA_HEAD_DIM = D_BRANCH // NA_HEADS
NA_WIN_H = 8
NA_WIN_W = 16
ROPE_BASE = 10000.0
EPS = 1e-6

F32 = jnp.float32
BF16 = jnp.bfloat16

LANES = 128
VMEM_LIMIT = 56 * 1024 * 1024

TM = TOK // 2
TN_IN = 1024
TM_OUT = TOK // 4
TN_OUT = 512
RET_CHUNK = 256
N_CHUNKS = TOK // RET_CHUNK

NA_QROWS = 4
NA_KROWS = 12
NA_QBLK = NA_QROWS * GRID_W
NA_KBLK = NA_KROWS * GRID_W
NA_NBLK = GRID_H // NA_QROWS
NEG_BIG = -1e30
LOG2E = 1.4426950408889634


def _dot(a, b):
    return jnp.dot(a, b, preferred_element_type=F32)


def _dot_nt(a, b):
    return lax.dot_general(a, b, (((1,), (1,)), ((), ())), preferred_element_type=F32)


def _dot_tn(a, b):
    return lax.dot_general(a, b, (((0,), (0,)), ((), ())), preferred_element_type=F32)


def _silu(x):
    return x * jax.nn.sigmoid(x)


def _is_ctx_rows(tile_idx, rows):
    tok = tile_idx * rows + lax.broadcasted_iota(jnp.int32, (rows, 1), 0)
    return tok < CTX_LEN


def _mod_kernel(c_ref, w_ref, b_ref, o_ref):
    cond = _silu(c_ref[...]).astype(BF16)
    o_ref[0] = _dot(cond, w_ref[0].astype(BF16)) + b_ref[0]


def _modulation(cvec, mod_w, mod_b):
    tn = 1024
    n = 3 * D_MODEL
    return pl.pallas_call(
        _mod_kernel,
        grid=(DEPTH, n // tn),
        in_specs=[
            pl.BlockSpec((8, D_MODEL), lambda l, j: (0, 0)),
            pl.BlockSpec((1, D_MODEL, tn), lambda l, j: (l, 0, j)),
            pl.BlockSpec((1, 1, tn), lambda l, j: (l, 0, j)),
        ],
        out_specs=pl.BlockSpec((1, 8, tn), lambda l, j: (l, 0, j)),
        out_shape=jax.ShapeDtypeStruct((DEPTH, 8, n), F32),
        compiler_params=pltpu.CompilerParams(
            dimension_semantics=("arbitrary", "arbitrary"), vmem_limit_bytes=VMEM_LIMIT),
        name="modulation",
    )(cvec, mod_w, mod_b.reshape(DEPTH, 1, n))


def _norm_mod_kernel(x_ref, g_ref, sc_ref, sh_ref, o_ref):
    row = jnp.where(pl.program_id(1) == 0, 0, 1)
    x = x_ref[0]
    y = x * lax.rsqrt(jnp.mean(x * x, axis=-1, keepdims=True) + EPS) * g_ref[...]
    sc = sc_ref[0, pl.ds(row, 1), :]
    sh = sh_ref[0, pl.ds(row, 1), :]
    o_ref[0] = (y * (1.0 + sc) + sh).astype(BF16)


def _norm_mod(xs, g, sc, sh):
    tm = CTX_LEN
    return pl.pallas_call(
        _norm_mod_kernel,
        grid=(BATCH, TOK // tm),
        in_specs=[
            pl.BlockSpec((1, tm, D_MODEL), lambda b, m: (b, m, 0)),
            pl.BlockSpec((1, D_MODEL), lambda b, m: (0, 0)),
            pl.BlockSpec((1, 2, D_MODEL), lambda b, m: (b, 0, 0)),
            pl.BlockSpec((1, 2, D_MODEL), lambda b, m: (b, 0, 0)),
        ],
        out_specs=pl.BlockSpec((1, tm, D_MODEL), lambda b, m: (b, m, 0)),
        out_shape=jax.ShapeDtypeStruct((BATCH, TOK, D_MODEL), BF16),
        compiler_params=pltpu.CompilerParams(
            dimension_semantics=("arbitrary", "arbitrary"), vmem_limit_bytes=VMEM_LIMIT),
        name="norm_mod",
    )(xs, g, sc, sh)


def _first_token_tile():
    return (pl.program_id(1) == 0) & (pl.program_id(2) == 0)


def _in_proj_kernel(h_ref, w_ref, cos_ref, sin_ref, o_ref, wb_scr, *, mode):
    j = pl.program_id(0)

    @pl.when(_first_token_tile())
    def _():
        wb_scr[...] = w_ref[0].astype(BF16)

    acc = _dot(h_ref[0], wb_scr[...])

    if mode == "ret":
        n_q = D_MODEL // TN_IN
        n_rope = 2 * D_MODEL // TN_IN

        @pl.when(j < n_rope)
        def _():
            scale = jnp.where(j < n_q, 1.0, RET_QK_DIM ** -0.5)
            for c0 in range(0, TN_IN, LANES):
                t0 = c0 % RET_QK_DIM
                xg = acc[:, c0:c0 + LANES]
                r = xg * cos_ref[:, t0:t0 + LANES] + pltpu.roll(xg, LANES // 2, 1) * sin_ref[:, t0:t0 + LANES]
                o_ref[0, :, c0:c0 + LANES] = (r * scale).astype(BF16)

        @pl.when(j >= n_rope)
        def _():
            o_ref[0] = acc.astype(BF16)
    else:
        n_q = D_BRANCH // TN_IN
        scale = jnp.where(j < n_q, NA_HEAD_DIM ** -0.5 * LOG2E, 1.0)
        o_ref[0] = (acc * scale).astype(BF16)


def _in_proj(h, w_all, layer, cos_t, sin_t, mode):
    n = w_all.shape[2]
    return pl.pallas_call(
        functools.partial(_in_proj_kernel, mode=mode),
        grid=(n // TN_IN, BATCH, TOK // TM),
        in_specs=[
            pl.BlockSpec((1, TM, D_MODEL), lambda j, b, m: (b, m, 0)),
            pl.BlockSpec((1, D_MODEL, TN_IN), lambda j, b, m: (layer, 0, j)),
            pl.BlockSpec((TM, RET_QK_DIM), lambda j, b, m: (m, 0)),
            pl.BlockSpec((TM, RET_QK_DIM), lambda j, b, m: (m, 0)),
        ],
        out_specs=pl.BlockSpec((1, TM, TN_IN), lambda j, b, m: (b, m, j)),
        out_shape=jax.ShapeDtypeStruct((BATCH, TOK, n), BF16),
        scratch_shapes=[pltpu.VMEM((D_MODEL, TN_IN), BF16)],
        compiler_params=pltpu.CompilerParams(
            dimension_semantics=("arbitrary", "arbitrary", "arbitrary"), vmem_limit_bytes=VMEM_LIMIT),
        name="in_proj_" + mode,
    )(h, w_all, cos_t, sin_t)


def _out_proj_kernel(a_ref, w_ref, x_ref, gt_ref, o_ref, wb_scr):
    @pl.when(_first_token_tile())
    def _():
        wb_scr[...] = w_ref[0].astype(BF16)

    y = _dot(a_ref[0], wb_scr[...])
    gt = jnp.where(_is_ctx_rows(pl.program_id(2), TM_OUT), gt_ref[0, 0:1, :], gt_ref[0, 1:2, :])
    o_ref[0] = x_ref[0] + gt * y


def _out_proj(a, w_all, layer, xs, gt):
    return pl.pallas_call(
        _out_proj_kernel,
        grid=(D_MODEL // TN_OUT, BATCH, TOK // TM_OUT),
        in_specs=[
            pl.BlockSpec((1, TM_OUT, D_BRANCH), lambda j, b, m: (b, m, 0)),
            pl.BlockSpec((1, D_BRANCH, TN_OUT), lambda j, b, m: (layer, 0, j)),
            pl.BlockSpec((1, TM_OUT, TN_OUT), lambda j, b, m: (b, m, j)),
            pl.BlockSpec((1, 2, TN_OUT), lambda j, b, m: (b, 0, j)),
        ],
        out_specs=pl.BlockSpec((1, TM_OUT, TN_OUT), lambda j, b, m: (b, m, j)),
        out_shape=jax.ShapeDtypeStruct((BATCH, TOK, D_MODEL), F32),
        scratch_shapes=[pltpu.VMEM((D_BRANCH, TN_OUT), BF16)],
        input_output_aliases={2: 0},
        compiler_params=pltpu.CompilerParams(
            dimension_semantics=("arbitrary", "arbitrary", "arbitrary"), vmem_limit_bytes=VMEM_LIMIT),
        name="out_proj",
    )(a, w_all, xs, gt)


def _ret_kernel(q_ref, k_ref, v_ref, g_ref, raw_ref, o_ref, s_scr, o_scr):
    c = RET_CHUNK
    raw = raw_ref[0]
    log_g = -(jnp.maximum(raw, 0.0) + jnp.log1p(jnp.exp(-jnp.abs(raw))))
    lg_f = log_g[0:1, 0:1]
    lg_b = log_g[1:2, 0:1]
    pos_i = lax.broadcasted_iota(jnp.int32, (c, 1), 0).astype(F32)
    ii = lax.broadcasted_iota(jnp.int32, (c, c), 0)
    jj = lax.broadcasted_iota(jnp.int32, (c, c), 1)
    diff = (ii - jj).astype(F32)

    def run(lg, reverse, order_fn, finalize):
        d = -diff if reverse else diff
        intra = jnp.where(d >= 0, jnp.exp(lg * jnp.maximum(d, 0.0)), 0.0)
        p = (c - 1.0 - pos_i) if reverse else pos_i
        q_dec = jnp.exp(lg * (p + 1.0))
        k_dec = jnp.exp(lg * (c - 1.0 - p))
        c_dec = jnp.exp(lg * float(c))
        s_scr[...] = jnp.zeros_like(s_scr)

        def step(t, carry):
            r0 = pl.multiple_of(order_fn(t) * c, c)
            qc = q_ref[0, pl.ds(r0, c), :]
            kc = k_ref[0, pl.ds(r0, c), :]
            vc = v_ref[0, pl.ds(r0, c), :]
            s = s_scr[...]
            scores = _dot_nt(qc, kc) * intra
            o = _dot(scores.astype(BF16), vc) + _dot(qc, s.astype(BF16)) * q_dec
            k_scaled = (kc.astype(F32) * k_dec).astype(BF16)
            s_scr[...] = s * c_dec + _dot_tn(k_scaled, vc)
            if not finalize:
                o_scr[pl.ds(r0, c), :] = o
            else:
                tot = o_scr[pl.ds(r0, c), :] + o
                nrm = tot * lax.rsqrt(jnp.mean(tot * tot, axis=-1, keepdims=True) + EPS)
                gate = _silu(g_ref[0, pl.ds(r0, c), :].astype(F32))
                o_ref[0, pl.ds(r0, c), :] = (gate * nrm).astype(BF16)
            return carry

        lax.fori_loop(0, N_CHUNKS, step, 0)

    run(lg_f, False, lambda t: t, False)
    run(lg_b, True, lambda t: jnp.where(t == 0, 0, N_CHUNKS - t), True)


def _ret_core(qkvg, raw):
    nq = D_MODEL // RET_QK_DIM
    nv = 2 * D_MODEL // RET_V_DIM
    ng = nv + RET_HEADS
    return pl.pallas_call(
        _ret_kernel,
        grid=(BATCH, RET_HEADS),
        in_specs=[
            pl.BlockSpec((1, TOK, RET_QK_DIM), lambda b, h: (b, 0, h)),
            pl.BlockSpec((1, TOK, RET_QK_DIM), lambda b, h: (b, 0, nq + h)),
            pl.BlockSpec((1, TOK, RET_V_DIM), lambda b, h: (b, 0, nv + h)),
            pl.BlockSpec((1, TOK, RET_V_DIM), lambda b, h: (b, 0, ng + h)),
            pl.BlockSpec((1, 2, LANES), lambda b, h: (h, 0, 0)),
        ],
        out_specs=pl.BlockSpec((1, TOK, RET_V_DIM), lambda b, h: (b, 0, h)),
        out_shape=jax.ShapeDtypeStruct((BATCH, TOK, D_BRANCH), BF16),
        scratch_shapes=[pltpu.VMEM((RET_QK_DIM, RET_V_DIM), F32), pltpu.VMEM((TOK, RET_V_DIM), F32)],
        compiler_params=pltpu.CompilerParams(
            dimension_semantics=("arbitrary", "arbitrary"), vmem_limit_bytes=VMEM_LIMIT),
        name="ret_core",
    )(qkvg, qkvg, qkvg, qkvg, raw)


def _na_key_start(i):
    return int(np.clip(NA_QROWS * i - NA_WIN_H // 2, 0, GRID_H - NA_KROWS))


def _na_row_geometry(i, a, j):
    r = NA_QROWS * i + a
    rs = int(np.clip(r - NA_WIN_H // 2, 0, GRID_H - NA_WIN_H))
    kr = _na_key_start(i) + j
    return rs <= kr < rs + NA_WIN_H, kr - r + NA_WIN_H - 1


NA_KIND_BLOCKS = (0, 1, NA_NBLK - 1)
for _i in range(2, NA_NBLK - 1):
    assert all(_na_row_geometry(_i, _a, _j) == _na_row_geometry(1, _a, _j)
               for _a in range(NA_QROWS) for _j in range(NA_KROWS))


def _rpb_pair_table(rpb):
    mid = NA_WIN_W - 1
    pad = jnp.zeros(rpb.shape[:2] + (LANES - rpb.shape[2],), rpb.dtype)
    centred = jnp.concatenate([rpb[..., mid:], pad, rpb[..., :mid]], axis=-1)
    shifted = jnp.roll(centred, GRID_W, axis=-1)
    lane = jnp.arange(LANES)
    around_64 = (lane >= GRID_W // 2) & (lane < LANES - GRID_W // 2)
    pair = jnp.where(around_64, shifted[:, 1:], centred[:, :-1])
    return jnp.concatenate([pair, jnp.zeros((rpb.shape[0], 2, LANES), rpb.dtype)], axis=1)


def _na_build_bias(pair_ref, bias_scr):
    c = lax.broadcasted_iota(jnp.int32, (GRID_W, LANES), 0)
    lane = lax.broadcasted_iota(jnp.int32, (GRID_W, LANES), 1)
    kc = lane & (GRID_W - 1)
    ws = jnp.clip(c - NA_WIN_W // 2, 0, GRID_W - NA_WIN_W)
    in_cols = (kc >= ws) & (kc < ws + NA_WIN_W)
    first = lane < GRID_W
    masks = {(True, True): in_cols,
             (True, False): in_cols & first,
             (False, True): in_cols & jnp.logical_not(first)}
    toeplitz = {}
    for kind, i in enumerate(NA_KIND_BLOCKS):
        for a in range(NA_QROWS):
            for grp in range(NA_KROWS // 2):
                v0, d0 = _na_row_geometry(i, a, 2 * grp)
                v1, _ = _na_row_geometry(i, a, 2 * grp + 1)
                if v0 or v1:
                    assert 0 <= d0 <= 2 * NA_WIN_H - 3
                    if d0 not in toeplitz:
                        base = jnp.broadcast_to(pair_ref[0, d0:d0 + 1, :], (GRID_W, LANES))
                        toeplitz[d0] = pltpu.roll(base, 0, 1, stride=1, stride_axis=0) * LOG2E
                    tile = jnp.where(masks[(v0, v1)], toeplitz[d0], NEG_BIG)
                else:
                    tile = jnp.full((GRID_W, LANES), NEG_BIG, F32)
                bias_scr[kind, a * GRID_W:(a + 1) * GRID_W, grp * LANES:(grp + 1) * LANES] = tile


def _na_kernel(q_ref, k_ref, v_ref, g_ref, pair_ref, o_ref, bias_ref, v1_scr):
    d = NA_HEAD_DIM

    @pl.when(pl.program_id(1) == 0)
    def _():
        _na_build_bias(pair_ref, bias_ref)
        v1_scr[:, d:] = jnp.ones((TOK, d), BF16)

    v1_scr[:, :d] = v_ref[0]
    kc = k_ref[0, 0:CTX_LEN, :]
    vc = v1_scr[0:CTX_LEN, :]

    def finish(o1, r0, rows):
        gate = _silu(g_ref[0, r0:r0 + rows, :].astype(F32))
        o_ref[0, r0:r0 + rows, :] = (gate * (o1[:, :d] / o1[:, d:])).astype(BF16)

    s = _dot_nt(q_ref[0, 0:CTX_LEN, :], kc)
    p = jnp.exp2(s - jnp.max(s, axis=-1, keepdims=True))
    finish(_dot(p.astype(BF16), vc), 0, CTX_LEN)

    for i in range(NA_NBLK):
        kind = 0 if i == 0 else (2 if i == NA_NBLK - 1 else 1)
        r0 = CTX_LEN + i * NA_QBLK
        k0 = CTX_LEN + _na_key_start(i) * GRID_W
        q = q_ref[0, r0:r0 + NA_QBLK, :]
        s_loc = _dot_nt(q, k_ref[0, k0:k0 + NA_KBLK, :]) + bias_ref[kind]
        s_ctx = _dot_nt(q, kc)
        m = jnp.maximum(jnp.max(s_loc, axis=-1, keepdims=True), jnp.max(s_ctx, axis=-1, keepdims=True))
        p_loc = jnp.exp2(s_loc - m).astype(BF16)
        p_ctx = jnp.exp2(s_ctx - m).astype(BF16)
        finish(_dot(p_loc, v1_scr[k0:k0 + NA_KBLK, :]) + _dot(p_ctx, vc), r0, NA_QBLK)


def _na_core(qkvg, rpb_pairs):
    nh = NA_HEADS
    return pl.pallas_call(
        _na_kernel,
        grid=(NA_HEADS, BATCH),
        in_specs=[
            pl.BlockSpec((1, TOK, NA_HEAD_DIM), lambda h, b: (b, 0, h)),
            pl.BlockSpec((1, TOK, NA_HEAD_DIM), lambda h, b: (b, 0, nh + h)),
            pl.BlockSpec((1, TOK, NA_HEAD_DIM), lambda h, b: (b, 0, 2 * nh + h)),
            pl.BlockSpec((1, TOK, NA_HEAD_DIM), lambda h, b: (b, 0, 3 * nh + h)),
            pl.BlockSpec((1, 2 * NA_WIN_H, LANES), lambda h, b: (h, 0, 0)),
        ],
        out_specs=pl.BlockSpec((1, TOK, NA_HEAD_DIM), lambda h, b: (b, 0, h)),
        out_shape=jax.ShapeDtypeStruct((BATCH, TOK, D_BRANCH), BF16),
        scratch_shapes=[pltpu.VMEM((len(NA_KIND_BLOCKS), NA_QBLK, NA_KBLK), F32),
                        pltpu.VMEM((TOK, 2 * NA_HEAD_DIM), BF16)],
        compiler_params=pltpu.CompilerParams(
            dimension_semantics=("arbitrary", "arbitrary"), vmem_limit_bytes=VMEM_LIMIT),
        name="na_core",
    )(qkvg, qkvg, qkvg, qkvg, rpb_pairs)


def _final_norm_kernel(x_ref, g_ref, o_ref):
    x = x_ref[0]
    o_ref[0] = x * lax.rsqrt(jnp.mean(x * x, axis=-1, keepdims=True) + EPS) * g_ref[...]


def _final_norm(xs, g):
    tm = CTX_LEN
    return pl.pallas_call(
        _final_norm_kernel,
        grid=(BATCH, SEQ // tm),
        in_specs=[
            pl.BlockSpec((1, tm, D_MODEL), lambda b, i: (b, i + CTX_LEN // tm, 0)),
            pl.BlockSpec((1, D_MODEL), lambda b, i: (0, 0)),
        ],
        out_specs=pl.BlockSpec((1, tm, D_MODEL), lambda b, i: (b, i, 0)),
        out_shape=jax.ShapeDtypeStruct((BATCH, SEQ, D_MODEL), F32),
        compiler_params=pltpu.CompilerParams(
            dimension_semantics=("arbitrary", "arbitrary"), vmem_limit_bytes=VMEM_LIMIT),
        name="final_norm",
    )(xs, g)


def _rope_tables():
    quarter = RET_QK_DIM // 4
    freqs = ROPE_BASE ** (-jnp.arange(quarter, dtype=F32) / quarter)
    t = jnp.arange(SEQ)
    ang_r = (t // GRID_W).astype(F32)[:, None] * freqs[None, :]
    ang_c = (t % GRID_W).astype(F32)[:, None] * freqs[None, :]
    cos = jnp.concatenate([jnp.cos(ang_r)] * 2 + [jnp.cos(ang_c)] * 2, axis=-1)
    sin = jnp.concatenate([-jnp.sin(ang_r), jnp.sin(ang_r), -jnp.sin(ang_c), jnp.sin(ang_c)], axis=-1)
    cos = jnp.concatenate([jnp.ones((CTX_LEN, RET_QK_DIM), F32), cos], axis=0)
    sin = jnp.concatenate([jnp.zeros((CTX_LEN, RET_QK_DIM), F32), sin], axis=0)
    return cos, sin


def _per_batch(m):
    ctx_row = jnp.broadcast_to(m[BATCH][None, None, :], (BATCH, 1, m.shape[-1]))
    return jnp.concatenate([ctx_row, m[:BATCH][:, None, :]], axis=1)


def kernel(x, c, ctx, c_ctx, mod_w, mod_b, norm_g, ret_w_in, ret_decay_fwd, ret_decay_bwd, ret_w_out,
           na_w_in, na_rpb, na_w_out, final_g):
    xs = jnp.concatenate([ctx, x], axis=1)
    cvec = jnp.concatenate([c, c_ctx[None], jnp.zeros((8 - BATCH - 1, D_MODEL), F32)], axis=0)
    mods = _modulation(cvec, mod_w, mod_b)
    cos_t, sin_t = _rope_tables()

    for l in range(DEPTH):
        j = l // 2
        sh = _per_batch(mods[l, :, :D_MODEL])
        sc = _per_batch(mods[l, :, D_MODEL:2 * D_MODEL])
        gt = _per_batch(mods[l, :, 2 * D_MODEL:])
        h = _norm_mod(xs, norm_g[l][None, :], sc, sh)
        if l % 2 == 0:
            qkvg = _in_proj(h, ret_w_in, j, cos_t, sin_t, "ret")
            raw = jnp.stack([ret_decay_fwd[j], ret_decay_bwd[j]], axis=1)
            raw = jnp.broadcast_to(raw[:, :, None], (RET_HEADS, 2, LANES))
            a = _ret_core(qkvg, raw)
            xs = _out_proj(a, ret_w_out, j, xs, gt)
        else:
            qkvg = _in_proj(h, na_w_in, j, cos_t, sin_t, "na")
            a = _na_core(qkvg, _rpb_pair_table(na_rpb[j]))
            xs = _out_proj(a, na_w_out, j, xs, gt)
    return _final_norm(xs, final_g[None, :])
```

```python
import functools

import numpy as np
import jax
import jax.numpy as jnp
from jax import lax
from jax.experimental import pallas as pl
from jax.experimental.pallas import tpu as pltpu

D_MODEL = 2048
BATCH = 4
SEQ = 2048
DEPTH = 4
GRID_W = 64
GRID_H = SEQ // GRID_W
CTX_LEN = 256
TOK = CTX_LEN + SEQ
D_BRANCH = 2 * D_MODEL
RET_HEADS = 8
RET_QK_DIM = D_MODEL // RET_HEADS
RET_V_DIM = D_BRANCH // RET_HEADS
NA_HEADS = 32
NA_HEAD_DIM = D_BRANCH // NA_HEADS
NA_WIN_H = 8
NA_WIN_W = 16
ROPE_BASE = 10000.0
EPS = 1e-6

F32 = jnp.float32
BF16 = jnp.bfloat16

LANES = 128
VMEM_LIMIT = 56 * 1024 * 1024

TM = TOK // 2
TN_IN = 1024
TM_OUT = TOK // 2
TN_OUT = 512
RET_CHUNK = 256
N_CHUNKS = TOK // RET_CHUNK

NA_QROWS = 4
NA_KROWS = 12
NA_QBLK = NA_QROWS * GRID_W
NA_KBLK = NA_KROWS * GRID_W
NA_NBLK = GRID_H // NA_QROWS
NEG_BIG = -1e30
LOG2E = 1.4426950408889634


def _dot(a, b):
    return jnp.dot(a, b, preferred_element_type=F32)


def _dot_nt(a, b):
    return lax.dot_general(a, b, (((1,), (1,)), ((), ())), preferred_element_type=F32)


def _dot_tn(a, b):
    return lax.dot_general(a, b, (((0,), (0,)), ((), ())), preferred_element_type=F32)


def _silu(x):
    return x * jax.nn.sigmoid(x)


def _is_ctx_rows(tile_idx, rows):
    tok = tile_idx * rows + lax.broadcasted_iota(jnp.int32, (rows, 1), 0)
    return tok < CTX_LEN


def _mod_kernel(c_ref, w_ref, b_ref, o_ref):
    cond = _silu(c_ref[...]).astype(BF16)
    o_ref[0] = _dot(cond, w_ref[0].astype(BF16)) + b_ref[0]


def _modulation(cvec, mod_w, mod_b):
    tn = 1024
    n = 3 * D_MODEL
    return pl.pallas_call(
        _mod_kernel,
        grid=(DEPTH, n // tn),
        in_specs=[
            pl.BlockSpec((8, D_MODEL), lambda l, j: (0, 0)),
            pl.BlockSpec((1, D_MODEL, tn), lambda l, j: (l, 0, j)),
            pl.BlockSpec((1, 1, tn), lambda l, j: (l, 0, j)),
        ],
        out_specs=pl.BlockSpec((1, 8, tn), lambda l, j: (l, 0, j)),
        out_shape=jax.ShapeDtypeStruct((DEPTH, 8, n), F32),
        compiler_params=pltpu.CompilerParams(
            dimension_semantics=("arbitrary", "arbitrary"), vmem_limit_bytes=VMEM_LIMIT),
        name="modulation",
    )(cvec, mod_w, mod_b.reshape(DEPTH, 1, n))


def _norm_mod_kernel(x_ref, g_ref, sc_ref, sh_ref, o_ref):
    row = jnp.where(pl.program_id(1) == 0, 0, 1)
    x = x_ref[0]
    y = x * lax.rsqrt(jnp.mean(x * x, axis=-1, keepdims=True) + EPS) * g_ref[...]
    sc = sc_ref[0, pl.ds(row, 1), :]
    sh = sh_ref[0, pl.ds(row, 1), :]
    o_ref[0] = (y * (1.0 + sc) + sh).astype(BF16)


def _norm_mod(xs, g, sc, sh):
    tm = CTX_LEN
    return pl.pallas_call(
        _norm_mod_kernel,
        grid=(BATCH, TOK // tm),
        in_specs=[
            pl.BlockSpec((1, tm, D_MODEL), lambda b, m: (b, m, 0)),
            pl.BlockSpec((1, D_MODEL), lambda b, m: (0, 0)),
            pl.BlockSpec((1, 2, D_MODEL), lambda b, m: (b, 0, 0)),
            pl.BlockSpec((1, 2, D_MODEL), lambda b, m: (b, 0, 0)),
        ],
        out_specs=pl.BlockSpec((1, tm, D_MODEL), lambda b, m: (b, m, 0)),
        out_shape=jax.ShapeDtypeStruct((BATCH, TOK, D_MODEL), BF16),
        compiler_params=pltpu.CompilerParams(
            dimension_semantics=("arbitrary", "arbitrary"), vmem_limit_bytes=VMEM_LIMIT),
        name="norm_mod",
    )(xs, g, sc, sh)


def _first_token_tile():
    return (pl.program_id(1) == 0) & (pl.program_id(2) == 0)


def _resident_matmul(h_ref, w_ref, wb_scr):
    @pl.when(_first_token_tile())
    def _():
        wb_scr[...] = w_ref[0].astype(BF16)

    return _dot(h_ref[0], wb_scr[...])


def _in_proj_scaled_kernel(h_ref, w_ref, o_ref, wb_scr, *, n_scaled, scale):
    acc = _resident_matmul(h_ref, w_ref, wb_scr)
    o_ref[0] = (acc * jnp.where(pl.program_id(0) < n_scaled, scale, 1.0)).astype(BF16)


def _in_proj_silu_kernel(h_ref, w_ref, o_ref, wb_scr):
    o_ref[0] = _silu(_resident_matmul(h_ref, w_ref, wb_scr)).astype(BF16)


def _in_proj_rope_kernel(h_ref, w_ref, cos_ref, sin_ref, o_ref, wb_scr):
    acc = _resident_matmul(h_ref, w_ref, wb_scr)
    scale = jnp.where(pl.program_id(0) < D_MODEL // TN_IN, 1.0, RET_QK_DIM ** -0.5)
    for c0 in range(0, TN_IN, LANES):
        t0 = c0 % RET_QK_DIM
        xg = acc[:, c0:c0 + LANES]
        r = xg * cos_ref[:, t0:t0 + LANES] + pltpu.roll(xg, LANES // 2, 1) * sin_ref[:, t0:t0 + LANES]
        o_ref[0, :, c0:c0 + LANES] = (r * scale).astype(BF16)


def _in_proj(body, name, h, w_all, layer, col0, ncols, tables=()):
    tile0 = col0 // TN_IN
    table_spec = pl.BlockSpec((TM, RET_QK_DIM), lambda j, b, m: (m, 0))
    return pl.pallas_call(
        body,
        grid=(ncols // TN_IN, BATCH, TOK // TM),
        in_specs=[
            pl.BlockSpec((1, TM, D_MODEL), lambda j, b, m: (b, m, 0)),
            pl.BlockSpec((1, D_MODEL, TN_IN), lambda j, b, m: (layer, 0, tile0 + j)),
        ] + [table_spec] * len(tables),
        out_specs=pl.BlockSpec((1, TM, TN_IN), lambda j, b, m: (b, m, j)),
        out_shape=jax.ShapeDtypeStruct((BATCH, TOK, ncols), BF16),
        scratch_shapes=[pltpu.VMEM((D_MODEL, TN_IN), BF16)],
        compiler_params=pltpu.CompilerParams(
            dimension_semantics=("arbitrary", "arbitrary", "arbitrary"), vmem_limit_bytes=VMEM_LIMIT),
        name=name,
    )(h, w_all, *tables)


def _out_proj_kernel(a_ref, w_ref, x_ref, gt_ref, o_ref, wb_scr):
    @pl.when(_first_token_tile())
    def _():
        wb_scr[...] = w_ref[0].astype(BF16)

    y = _dot(a_ref[0], wb_scr[...])
    gt = jnp.where(_is_ctx_rows(pl.program_id(2), TM_OUT), gt_ref[0, 0:1, :], gt_ref[0, 1:2, :])
    o_ref[0] = x_ref[0] + gt * y


def _out_proj(a, w_all, layer, xs, gt):
    return pl.pallas_call(
        _out_proj_kernel,
        grid=(D_MODEL // TN_OUT, BATCH, TOK // TM_OUT),
        in_specs=[
            pl.BlockSpec((1, TM_OUT, D_BRANCH), lambda j, b, m: (b, m, 0)),
            pl.BlockSpec((1, D_BRANCH, TN_OUT), lambda j, b, m: (layer, 0, j)),
            pl.BlockSpec((1, TM_OUT, TN_OUT), lambda j, b, m: (b, m, j)),
            pl.BlockSpec((1, 2, TN_OUT), lambda j, b, m: (b, 0, j)),
        ],
        out_specs=pl.BlockSpec((1, TM_OUT, TN_OUT), lambda j, b, m: (b, m, j)),
        out_shape=jax.ShapeDtypeStruct((BATCH, TOK, D_MODEL), F32),
        scratch_shapes=[pltpu.VMEM((D_BRANCH, TN_OUT), BF16)],
        input_output_aliases={2: 0},
        compiler_params=pltpu.CompilerParams(
            dimension_semantics=("arbitrary", "arbitrary", "arbitrary"), vmem_limit_bytes=VMEM_LIMIT),
        name="out_proj",
    )(a, w_all, xs, gt)


def _ret_kernel(q_ref, k_ref, v_ref, g_ref, raw_ref, o_ref, of_scr, ob_scr):
    c = RET_CHUNK
    raw = raw_ref[0]
    log_g = -(jnp.maximum(raw, 0.0) + jnp.log1p(jnp.exp(-jnp.abs(raw))))
    pos_i = lax.broadcasted_iota(jnp.int32, (c, 1), 0).astype(F32)
    ii = lax.broadcasted_iota(jnp.int32, (c, c), 0)
    jj = lax.broadcasted_iota(jnp.int32, (c, c), 1)
    diff = (ii - jj).astype(F32)

    def decays(lg, reverse):
        d = -diff if reverse else diff
        intra = jnp.where(d >= 0, jnp.exp(lg * jnp.maximum(d, 0.0)), 0.0)
        p = (c - 1.0 - pos_i) if reverse else pos_i
        q_dec = jnp.exp(lg * (p + 1.0))
        k_dec = jnp.exp(lg * (c - 1.0 - p))
        c_dec = jnp.exp(lg * float(c))
        return intra, q_dec, k_dec, c_dec

    dec_f = decays(log_g[0:1, 0:1], False)
    dec_b = decays(log_g[1:2, 0:1], True)

    def rows(chunk):
        return pl.ds(chunk * c, c)

    def scan_chunk(chunk, s, dec):
        intra, q_dec, k_dec, c_dec = dec
        qc = q_ref[0, rows(chunk), :]
        kc = k_ref[0, rows(chunk), :]
        vc = v_ref[0, rows(chunk), :]
        scores = (_dot_nt(qc, kc) * intra).astype(BF16)
        k_scaled = (kc.astype(F32) * k_dec).astype(BF16)
        if s is None:
            return _dot(scores, vc), _dot_tn(k_scaled, vc)
        q_scaled = (qc.astype(F32) * q_dec).astype(BF16)
        return _dot(scores, vc) + _dot(q_scaled, s.astype(BF16)), s * c_dec + _dot_tn(k_scaled, vc)

    def finalize(chunk, tot):
        nrm = tot * lax.rsqrt(jnp.mean(tot * tot, axis=-1, keepdims=True) + EPS)
        o_ref[0, rows(chunk), :] = (g_ref[0, rows(chunk), :].astype(F32) * nrm).astype(BF16)

    fwd_order = list(range(N_CHUNKS))
    bwd_order = [0] + list(range(N_CHUNKS - 1, 0, -1))
    s_f = s_b = None
    fwd_seen, bwd_seen = set(), set()
    for cf, cb in zip(fwd_order, bwd_order):
        o_f, s_f = scan_chunk(cf, s_f, dec_f)
        o_b, s_b = scan_chunk(cb, s_b, dec_b)
        fwd_seen.add(cf)
        bwd_seen.add(cb)
        if cf == cb:
            finalize(cf, o_f + o_b)
            continue
        if cf in bwd_seen:
            finalize(cf, o_f + ob_scr[rows(cf), :])
        else:
            of_scr[rows(cf), :] = o_f
        if cb in fwd_seen:
            finalize(cb, of_scr[rows(cb), :] + o_b)
        else:
            ob_scr[rows(cb), :] = o_b


def _ret_core(qk, v, g, raw):
    nh = RET_HEADS
    return pl.pallas_call(
        _ret_kernel,
        grid=(BATCH, RET_HEADS),
        in_specs=[
            pl.BlockSpec((1, TOK, RET_QK_DIM), lambda b, h: (b, 0, h)),
            pl.BlockSpec((1, TOK, RET_QK_DIM), lambda b, h: (b, 0, nh + h)),
            pl.BlockSpec((1, TOK, RET_V_DIM), lambda b, h: (b, 0, h)),
            pl.BlockSpec((1, TOK, RET_V_DIM), lambda b, h: (b, 0, h)),
            pl.BlockSpec((1, 2, LANES), lambda b, h: (h, 0, 0)),
        ],
        out_specs=pl.BlockSpec((1, TOK, RET_V_DIM), lambda b, h: (b, 0, h)),
        out_shape=jax.ShapeDtypeStruct((BATCH, TOK, D_BRANCH), BF16),
        scratch_shapes=[pltpu.VMEM((TOK, RET_V_DIM), F32), pltpu.VMEM((TOK, RET_V_DIM), F32)],
        compiler_params=pltpu.CompilerParams(
            dimension_semantics=("arbitrary", "arbitrary"), vmem_limit_bytes=VMEM_LIMIT),
        name="ret_core",
    )(qk, qk, v, g, raw)


def _na_key_start(i):
    return int(np.clip(NA_QROWS * i - NA_WIN_H // 2, 0, GRID_H - NA_KROWS))


def _na_row_geometry(i, a, j):
    r = NA_QROWS * i + a
    rs = int(np.clip(r - NA_WIN_H // 2, 0, GRID_H - NA_WIN_H))
    kr = _na_key_start(i) + j
    return rs <= kr < rs + NA_WIN_H, kr - r + NA_WIN_H - 1


NA_KIND_BLOCKS = (0, 1, NA_NBLK - 1)
for _i in range(2, NA_NBLK - 1):
    assert all(_na_row_geometry(_i, _a, _j) == _na_row_geometry(1, _a, _j)
               for _a in range(NA_QROWS) for _j in range(NA_KROWS))


def _rpb_pair_table(rpb):
    mid = NA_WIN_W - 1
    pad = jnp.zeros(rpb.shape[:2] + (LANES - rpb.shape[2],), rpb.dtype)
    centred = jnp.concatenate([rpb[..., mid:], pad, rpb[..., :mid]], axis=-1)
    shifted = jnp.roll(centred, GRID_W, axis=-1)
    lane = jnp.arange(LANES)
    around_64 = (lane >= GRID_W // 2) & (lane < LANES - GRID_W // 2)
    pair = jnp.where(around_64, shifted[:, 1:], centred[:, :-1])
    return jnp.concatenate([pair, jnp.zeros((rpb.shape[0], 2, LANES), rpb.dtype)], axis=1)


def _na_build_bias(pair_ref, bias_scr):
    c = lax.broadcasted_iota(jnp.int32, (GRID_W, LANES), 0)
    lane = lax.broadcasted_iota(jnp.int32, (GRID_W, LANES), 1)
    kc = lane & (GRID_W - 1)
    ws = jnp.clip(c - NA_WIN_W // 2, 0, GRID_W - NA_WIN_W)
    in_cols = (kc >= ws) & (kc < ws + NA_WIN_W)
    first = lane < GRID_W
    masks = {(True, True): in_cols,
             (True, False): in_cols & first,
             (False, True): in_cols & jnp.logical_not(first)}
    toeplitz = {}
    for kind, i in enumerate(NA_KIND_BLOCKS):
        for a in range(NA_QROWS):
            for grp in range(NA_KROWS // 2):
                v0, d0 = _na_row_geometry(i, a, 2 * grp)
                v1, _ = _na_row_geometry(i, a, 2 * grp + 1)
                if v0 or v1:
                    assert 0 <= d0 <= 2 * NA_WIN_H - 3
                    if d0 not in toeplitz:
                        base = jnp.broadcast_to(pair_ref[0, d0:d0 + 1, :], (GRID_W, LANES))
                        toeplitz[d0] = pltpu.roll(base, 0, 1, stride=1, stride_axis=0) * LOG2E
                    tile = jnp.where(masks[(v0, v1)], toeplitz[d0], NEG_BIG)
                else:
                    tile = jnp.full((GRID_W, LANES), NEG_BIG, F32)
                bias_scr[kind, a * GRID_W:(a + 1) * GRID_W, grp * LANES:(grp + 1) * LANES] = tile


def _na_kernel(q_ref, k_ref, v_ref, g_ref, pair_ref, o_ref, bias_ref, v1_scr):
    d = NA_HEAD_DIM

    @pl.when(pl.program_id(1) == 0)
    def _():
        _na_build_bias(pair_ref, bias_ref)
        v1_scr[:, d:] = jnp.ones((TOK, d), BF16)

    v1_scr[:, :d] = v_ref[0]
    kc = k_ref[0, 0:CTX_LEN, :]
    vc = v1_scr[0:CTX_LEN, :]

    def finish(o1, r0, rows):
        gate = g_ref[0, r0:r0 + rows, :].astype(F32)
        o_ref[0, r0:r0 + rows, :] = (gate * (o1[:, :d] / o1[:, d:])).astype(BF16)

    s = _dot_nt(q_ref[0, 0:CTX_LEN, :], kc)
    p = jnp.exp2(s - jnp.max(s, axis=-1, keepdims=True))
    finish(_dot(p.astype(BF16), vc), 0, CTX_LEN)

    for i in range(NA_NBLK):
        kind = 0 if i == 0 else (2 if i == NA_NBLK - 1 else 1)
        r0 = CTX_LEN + i * NA_QBLK
        k0 = CTX_LEN + _na_key_start(i) * GRID_W
        q = q_ref[0, r0:r0 + NA_QBLK, :]
        s_loc = _dot_nt(q, k_ref[0, k0:k0 + NA_KBLK, :]) + bias_ref[kind]
        s_ctx = _dot_nt(q, kc)
        m = jnp.maximum(jnp.max(s_loc, axis=-1, keepdims=True), jnp.max(s_ctx, axis=-1, keepdims=True))
        p_loc = jnp.exp2(s_loc - m).astype(BF16)
        p_ctx = jnp.exp2(s_ctx - m).astype(BF16)
        finish(_dot(p_loc, v1_scr[k0:k0 + NA_KBLK, :]) + _dot(p_ctx, vc), r0, NA_QBLK)


def _na_core(qkv, g, rpb_pairs):
    nh = NA_HEADS
    return pl.pallas_call(
        _na_kernel,
        grid=(NA_HEADS, BATCH),
        in_specs=[
            pl.BlockSpec((1, TOK, NA_HEAD_DIM), lambda h, b: (b, 0, h)),
            pl.BlockSpec((1, TOK, NA_HEAD_DIM), lambda h, b: (b, 0, nh + h)),
            pl.BlockSpec((1, TOK, NA_HEAD_DIM), lambda h, b: (b, 0, 2 * nh + h)),
            pl.BlockSpec((1, TOK, NA_HEAD_DIM), lambda h, b: (b, 0, h)),
            pl.BlockSpec((1, 2 * NA_WIN_H, LANES), lambda h, b: (h, 0, 0)),
        ],
        out_specs=pl.BlockSpec((1, TOK, NA_HEAD_DIM), lambda h, b: (b, 0, h)),
        out_shape=jax.ShapeDtypeStruct((BATCH, TOK, D_BRANCH), BF16),
        scratch_shapes=[pltpu.VMEM((len(NA_KIND_BLOCKS), NA_QBLK, NA_KBLK), F32),
                        pltpu.VMEM((TOK, 2 * NA_HEAD_DIM), BF16)],
        compiler_params=pltpu.CompilerParams(
            dimension_semantics=("arbitrary", "arbitrary"), vmem_limit_bytes=VMEM_LIMIT),
        name="na_core",
    )(qkv, qkv, qkv, g, rpb_pairs)


def _final_norm_kernel(x_ref, g_ref, o_ref):
    x = x_ref[0]
    o_ref[0] = x * lax.rsqrt(jnp.mean(x * x, axis=-1, keepdims=True) + EPS) * g_ref[...]


def _final_norm(xs, g):
    tm = CTX_LEN
    return pl.pallas_call(
        _final_norm_kernel,
        grid=(BATCH, SEQ // tm),
        in_specs=[
            pl.BlockSpec((1, tm, D_MODEL), lambda b, i: (b, i + CTX_LEN // tm, 0)),
            pl.BlockSpec((1, D_MODEL), lambda b, i: (0, 0)),
        ],
        out_specs=pl.BlockSpec((1, tm, D_MODEL), lambda b, i: (b, i, 0)),
        out_shape=jax.ShapeDtypeStruct((BATCH, SEQ, D_MODEL), F32),
        compiler_params=pltpu.CompilerParams(
            dimension_semantics=("arbitrary", "arbitrary"), vmem_limit_bytes=VMEM_LIMIT),
        name="final_norm",
    )(xs, g)


def _rope_tables():
    quarter = RET_QK_DIM // 4
    freqs = ROPE_BASE ** (-jnp.arange(quarter, dtype=F32) / quarter)
    t = jnp.arange(SEQ)
    ang_r = (t // GRID_W).astype(F32)[:, None] * freqs[None, :]
    ang_c = (t % GRID_W).astype(F32)[:, None] * freqs[None, :]
    cos = jnp.concatenate([jnp.cos(ang_r)] * 2 + [jnp.cos(ang_c)] * 2, axis=-1)
    sin = jnp.concatenate([-jnp.sin(ang_r), jnp.sin(ang_r), -jnp.sin(ang_c), jnp.sin(ang_c)], axis=-1)
    cos = jnp.concatenate([jnp.ones((CTX_LEN, RET_QK_DIM), F32), cos], axis=0)
    sin = jnp.concatenate([jnp.zeros((CTX_LEN, RET_QK_DIM), F32), sin], axis=0)
    return cos, sin


def _per_batch(m):
    ctx_row = jnp.broadcast_to(m[BATCH][None, None, :], (BATCH, 1, m.shape[-1]))
    return jnp.concatenate([ctx_row, m[:BATCH][:, None, :]], axis=1)


def kernel(x, c, ctx, c_ctx, mod_w, mod_b, norm_g, ret_w_in, ret_decay_fwd, ret_decay_bwd, ret_w_out,
           na_w_in, na_rpb, na_w_out, final_g):
    xs = jnp.concatenate([ctx, x], axis=1)
    cvec = jnp.concatenate([c, c_ctx[None], jnp.zeros((8 - BATCH - 1, D_MODEL), F32)], axis=0)
    mods = _modulation(cvec, mod_w, mod_b)
    cos_t, sin_t = _rope_tables()

    for l in range(DEPTH):
        j = l // 2
        sh = _per_batch(mods[l, :, :D_MODEL])
        sc = _per_batch(mods[l, :, D_MODEL:2 * D_MODEL])
        gt = _per_batch(mods[l, :, 2 * D_MODEL:])
        h = _norm_mod(xs, norm_g[l][None, :], sc, sh)
        if l % 2 == 0:
            qk = _in_proj(_in_proj_rope_kernel, "in_proj_ret_qk", h, ret_w_in, j, 0, 2 * D_MODEL, (cos_t, sin_t))
            plain = functools.partial(_in_proj_scaled_kernel, n_scaled=0, scale=1.0)
            v = _in_proj(plain, "in_proj_ret_v", h, ret_w_in, j, 2 * D_MODEL, D_BRANCH)
            g = _in_proj(_in_proj_silu_kernel, "in_proj_ret_g", h, ret_w_in, j, 2 * D_MODEL + D_BRANCH, D_BRANCH)
            raw = jnp.stack([ret_decay_fwd[j], ret_decay_bwd[j]], axis=1)
            raw = jnp.broadcast_to(raw[:, :, None], (RET_HEADS, 2, LANES))
            a = _ret_core(qk, v, g, raw)
            xs = _out_proj(a, ret_w_out, j, xs, gt)
        else:
            scaled_q = functools.partial(_in_proj_scaled_kernel, n_scaled=D_BRANCH // TN_IN,
                                         scale=NA_HEAD_DIM ** -0.5 * LOG2E)
            qkv = _in_proj(scaled_q, "in_proj_na_qkv", h, na_w_in, j, 0, 3 * D_BRANCH)
            g = _in_proj(_in_proj_silu_kernel, "in_proj_na_g", h, na_w_in, j, 3 * D_BRANCH, D_BRANCH)
            a = _na_core(qkv, g, _rpb_pair_table(na_rpb[j]))
            xs = _out_proj(a, na_w_out, j, xs, gt)
    return _final_norm(xs, final_g[None, :])
```

```python
import functools

import numpy as np
import jax
import jax.numpy as jnp
from jax import lax
from jax.experimental import pallas as pl
from jax.experimental.pallas import tpu as pltpu

D_MODEL = 2048
BATCH = 4
SEQ = 2048
DEPTH = 4
GRID_W = 64
GRID_H = SEQ // GRID_W
CTX_LEN = 256
TOK = CTX_LEN + SEQ
D_BRANCH = 2 * D_MODEL
RET_HEADS = 8
RET_QK_DIM = D_MODEL // RET_HEADS
RET_V_DIM = D_BRANCH // RET_HEADS
NA_HEADS = 32
NA_HEAD_DIM = D_BRANCH // NA_HEADS
NA_WIN_H = 8
NA_WIN_W = 16
ROPE_BASE = 10000.0
EPS = 1e-6

F32 = jnp.float32
BF16 = jnp.bfloat16

LANES = 128
VMEM_LIMIT = 56 * 1024 * 1024

TM = TOK // 2
TN_IN = 1024
TM_OUT = TOK // 2
TN_OUT = 512
TM_NORM = TOK // 3
NORM_SLAB = 32
RET_CHUNK = 256
N_CHUNKS = TOK // RET_CHUNK

NA_QROWS = 4
NA_KROWS = 12
NA_QBLK = NA_QROWS * GRID_W
NA_KBLK = NA_KROWS * GRID_W
NA_NBLK = GRID_H // NA_QROWS
NA_LOOKAHEAD = 4
NEG_BIG = -1e30
LOG2E = 1.4426950408889634


def _dot(a, b):
    return jnp.dot(a, b, preferred_element_type=F32)


def _dot_nt(a, b):
    return lax.dot_general(a, b, (((1,), (1,)), ((), ())), preferred_element_type=F32)


def _dot_tn(a, b):
    return lax.dot_general(a, b, (((0,), (0,)), ((), ())), preferred_element_type=F32)


def _silu(x):
    return x * jax.nn.sigmoid(x)


def _is_ctx_rows(tile_idx, rows):
    tok = tile_idx * rows + lax.broadcasted_iota(jnp.int32, (rows, 1), 0)
    return tok < CTX_LEN


def _mod_kernel(c_ref, w_ref, b_ref, o_ref):
    cond = _silu(c_ref[...]).astype(BF16)
    o_ref[0] = _dot(cond, w_ref[0].astype(BF16)) + b_ref[0]


def _modulation(cvec, mod_w, mod_b):
    tn = 1024
    n = 3 * D_MODEL
    return pl.pallas_call(
        _mod_kernel,
        grid=(DEPTH, n // tn),
        in_specs=[
            pl.BlockSpec((8, D_MODEL), lambda l, j: (0, 0)),
            pl.BlockSpec((1, D_MODEL, tn), lambda l, j: (l, 0, j)),
            pl.BlockSpec((1, 1, tn), lambda l, j: (l, 0, j)),
        ],
        out_specs=pl.BlockSpec((1, 8, tn), lambda l, j: (l, 0, j)),
        out_shape=jax.ShapeDtypeStruct((DEPTH, 8, n), F32),
        compiler_params=pltpu.CompilerParams(
            dimension_semantics=("arbitrary", "arbitrary"), vmem_limit_bytes=VMEM_LIMIT),
        name="modulation",
    )(cvec, mod_w, mod_b.reshape(DEPTH, 1, n))


def _norm_mod_kernel(x_ref, g_ref, sc_ref, sh_ref, o_ref):
    ctx_row = jnp.where(pl.program_id(1) == 0, 0, 1)
    g = g_ref[...]
    gain = [g * (1.0 + sc_ref[0, pl.ds(row, 1), :]) for row in (ctx_row, 1)]
    shift = [sh_ref[0, pl.ds(row, 1), :] for row in (ctx_row, 1)]
    for r0 in range(0, TM_NORM, NORM_SLAB):
        which = 0 if r0 < CTX_LEN else 1
        x = x_ref[0, r0:r0 + NORM_SLAB, :]
        y = x * lax.rsqrt(jnp.mean(x * x, axis=-1, keepdims=True) + EPS)
        o_ref[0, r0:r0 + NORM_SLAB, :] = (y * gain[which] + shift[which]).astype(BF16)


def _norm_mod(xs, g, sc, sh):
    tm = TM_NORM
    return pl.pallas_call(
        _norm_mod_kernel,
        grid=(BATCH, TOK // tm),
        in_specs=[
            pl.BlockSpec((1, tm, D_MODEL), lambda b, m: (b, m, 0)),
            pl.BlockSpec((1, D_MODEL), lambda b, m: (0, 0)),
            pl.BlockSpec((1, 2, D_MODEL), lambda b, m: (b, 0, 0)),
            pl.BlockSpec((1, 2, D_MODEL), lambda b, m: (b, 0, 0)),
        ],
        out_specs=pl.BlockSpec((1, tm, D_MODEL), lambda b, m: (b, m, 0)),
        out_shape=jax.ShapeDtypeStruct((BATCH, TOK, D_MODEL), BF16),
        compiler_params=pltpu.CompilerParams(
            dimension_semantics=("arbitrary", "arbitrary"), vmem_limit_bytes=VMEM_LIMIT),
        name="norm_mod",
    )(xs, g, sc, sh)


def _first_token_tile():
    return (pl.program_id(1) == 0) & (pl.program_id(2) == 0)


def _resident_matmul(h_ref, w_ref, wb_scr):
    @pl.when(_first_token_tile())
    def _():
        wb_scr[...] = w_ref[0].astype(BF16)

    return _dot(h_ref[0], wb_scr[...])


def _in_proj_scaled_kernel(h_ref, w_ref, o_ref, wb_scr, *, n_scaled, scale):
    acc = _resident_matmul(h_ref, w_ref, wb_scr)
    o_ref[0] = (acc * jnp.where(pl.program_id(0) < n_scaled, scale, 1.0)).astype(BF16)


def _in_proj_silu_kernel(h_ref, w_ref, o_ref, wb_scr):
    o_ref[0] = _silu(_resident_matmul(h_ref, w_ref, wb_scr)).astype(BF16)


def _in_proj_rope_kernel(h_ref, w_ref, cos_ref, sin_ref, o_ref, wb_scr):
    acc = _resident_matmul(h_ref, w_ref, wb_scr)
    scale = jnp.where(pl.program_id(0) < D_MODEL // TN_IN, 1.0, RET_QK_DIM ** -0.5)
    for c0 in range(0, TN_IN, LANES):
        t0 = c0 % RET_QK_DIM
        xg = acc[:, c0:c0 + LANES]
        r = xg * cos_ref[:, t0:t0 + LANES] + pltpu.roll(xg, LANES // 2, 1) * sin_ref[:, t0:t0 + LANES]
        o_ref[0, :, c0:c0 + LANES] = (r * scale).astype(BF16)


def _in_proj(body, name, h, w_all, layer, col0, ncols, tables=()):
    tile0 = col0 // TN_IN
    table_spec = pl.BlockSpec((TM, RET_QK_DIM), lambda j, b, m: (m, 0))
    return pl.pallas_call(
        body,
        grid=(ncols // TN_IN, BATCH, TOK // TM),
        in_specs=[
            pl.BlockSpec((1, TM, D_MODEL), lambda j, b, m: (b, m, 0)),
            pl.BlockSpec((1, D_MODEL, TN_IN), lambda j, b, m: (layer, 0, tile0 + j)),
        ] + [table_spec] * len(tables),
        out_specs=pl.BlockSpec((1, TM, TN_IN), lambda j, b, m: (b, m, j)),
        out_shape=jax.ShapeDtypeStruct((BATCH, TOK, ncols), BF16),
        scratch_shapes=[pltpu.VMEM((D_MODEL, TN_IN), BF16)],
        compiler_params=pltpu.CompilerParams(
            dimension_semantics=("arbitrary", "arbitrary", "arbitrary"), vmem_limit_bytes=VMEM_LIMIT),
        name=name,
    )(h, w_all, *tables)


def _out_proj_kernel(a_ref, w_ref, x_ref, gt_ref, o_ref, wb_scr):
    @pl.when(_first_token_tile())
    def _():
        wb_scr[...] = w_ref[0].astype(BF16)

    y = _dot(a_ref[0], wb_scr[...])
    gt = jnp.where(_is_ctx_rows(pl.program_id(2), TM_OUT), gt_ref[0, 0:1, :], gt_ref[0, 1:2, :])
    o_ref[0] = x_ref[0] + gt * y


def _out_proj(a, w_all, layer, xs, gt):
    return pl.pallas_call(
        _out_proj_kernel,
        grid=(D_MODEL // TN_OUT, BATCH, TOK // TM_OUT),
        in_specs=[
            pl.BlockSpec((1, TM_OUT, D_BRANCH), lambda j, b, m: (b, m, 0)),
            pl.BlockSpec((1, D_BRANCH, TN_OUT), lambda j, b, m: (layer, 0, j)),
            pl.BlockSpec((1, TM_OUT, TN_OUT), lambda j, b, m: (b, m, j)),
            pl.BlockSpec((1, 2, TN_OUT), lambda j, b, m: (b, 0, j)),
        ],
        out_specs=pl.BlockSpec((1, TM_OUT, TN_OUT), lambda j, b, m: (b, m, j)),
        out_shape=jax.ShapeDtypeStruct((BATCH, TOK, D_MODEL), F32),
        scratch_shapes=[pltpu.VMEM((D_BRANCH, TN_OUT), BF16)],
        input_output_aliases={2: 0},
        compiler_params=pltpu.CompilerParams(
            dimension_semantics=("arbitrary", "arbitrary", "arbitrary"), vmem_limit_bytes=VMEM_LIMIT),
        name="out_proj",
    )(a, w_all, xs, gt)


def _ret_kernel(q_ref, k_ref, v_ref, g_ref, raw_ref, o_ref, of_scr, ob_scr):
    c = RET_CHUNK
    raw = raw_ref[0]
    log_g = -(jnp.maximum(raw, 0.0) + jnp.log1p(jnp.exp(-jnp.abs(raw))))
    pos_i = lax.broadcasted_iota(jnp.int32, (c, 1), 0).astype(F32)
    ii = lax.broadcasted_iota(jnp.int32, (c, c), 0)
    jj = lax.broadcasted_iota(jnp.int32, (c, c), 1)
    diff = (ii - jj).astype(F32)

    def decays(lg, reverse):
        d = -diff if reverse else diff
        intra = jnp.where(d >= 0, jnp.exp(lg * jnp.maximum(d, 0.0)), 0.0)
        p = (c - 1.0 - pos_i) if reverse else pos_i
        q_dec = jnp.exp(lg * (p + 1.0))
        k_dec = jnp.exp(lg * (c - 1.0 - p))
        c_dec = jnp.exp(lg * float(c))
        return intra, q_dec, k_dec, c_dec

    dec_f = decays(log_g[0:1, 0:1], False)
    dec_b = decays(log_g[1:2, 0:1], True)

    def rows(chunk):
        return pl.ds(chunk * c, c)

    def scan_chunk(chunk, s, dec):
        intra, q_dec, k_dec, c_dec = dec
        qc = q_ref[0, rows(chunk), :]
        kc = k_ref[0, rows(chunk), :]
        vc = v_ref[0, rows(chunk), :]
        scores = (_dot_nt(qc, kc) * intra).astype(BF16)
        k_scaled = (kc.astype(F32) * k_dec).astype(BF16)
        if s is None:
            return _dot(scores, vc), _dot_tn(k_scaled, vc)
        q_scaled = (qc.astype(F32) * q_dec).astype(BF16)
        return _dot(scores, vc) + _dot(q_scaled, s.astype(BF16)), s * c_dec + _dot_tn(k_scaled, vc)

    def finalize(chunk, tot):
        nrm = tot * lax.rsqrt(jnp.mean(tot * tot, axis=-1, keepdims=True) + EPS)
        o_ref[0, rows(chunk), :] = (g_ref[0, rows(chunk), :].astype(F32) * nrm).astype(BF16)

    fwd_order = list(range(N_CHUNKS))
    bwd_order = [0] + list(range(N_CHUNKS - 1, 0, -1))
    s_f = s_b = None
    fwd_seen, bwd_seen = set(), set()
    for cf, cb in zip(fwd_order, bwd_order):
        o_f, s_f = scan_chunk(cf, s_f, dec_f)
        o_b, s_b = scan_chunk(cb, s_b, dec_b)
        fwd_seen.add(cf)
        bwd_seen.add(cb)
        if cf == cb:
            finalize(cf, o_f + o_b)
            continue
        if cf in bwd_seen:
            finalize(cf, o_f + ob_scr[rows(cf), :])
        else:
            of_scr[rows(cf), :] = o_f
        if cb in fwd_seen:
            finalize(cb, of_scr[rows(cb), :] + o_b)
        else:
            ob_scr[rows(cb), :] = o_b


def _ret_core(qk, v, g, raw):
    nh = RET_HEADS
    return pl.pallas_call(
        _ret_kernel,
        grid=(BATCH, RET_HEADS),
        in_specs=[
            pl.BlockSpec((1, TOK, RET_QK_DIM), lambda b, h: (b, 0, h)),
            pl.BlockSpec((1, TOK, RET_QK_DIM), lambda b, h: (b, 0, nh + h)),
            pl.BlockSpec((1, TOK, RET_V_DIM), lambda b, h: (b, 0, h)),
            pl.BlockSpec((1, TOK, RET_V_DIM), lambda b, h: (b, 0, h)),
            pl.BlockSpec((1, 2, LANES), lambda b, h: (h, 0, 0)),
        ],
        out_specs=pl.BlockSpec((1, TOK, RET_V_DIM), lambda b, h: (b, 0, h)),
        out_shape=jax.ShapeDtypeStruct((BATCH, TOK, D_BRANCH), BF16),
        scratch_shapes=[pltpu.VMEM((TOK, RET_V_DIM), F32), pltpu.VMEM((TOK, RET_V_DIM), F32)],
        compiler_params=pltpu.CompilerParams(
            dimension_semantics=("arbitrary", "arbitrary"), vmem_limit_bytes=VMEM_LIMIT),
        name="ret_core",
    )(qk, qk, v, g, raw)


def _na_key_start(i):
    return int(np.clip(NA_QROWS * i - NA_WIN_H // 2, 0, GRID_H - NA_KROWS))


def _na_row_geometry(i, a, j):
    r = NA_QROWS * i + a
    rs = int(np.clip(r - NA_WIN_H // 2, 0, GRID_H - NA_WIN_H))
    kr = _na_key_start(i) + j
    return rs <= kr < rs + NA_WIN_H, kr - r + NA_WIN_H - 1


NA_KIND_BLOCKS = (0, 1, NA_NBLK - 1)
for _i in range(2, NA_NBLK - 1):
    assert all(_na_row_geometry(_i, _a, _j) == _na_row_geometry(1, _a, _j)
               for _a in range(NA_QROWS) for _j in range(NA_KROWS))


def _rpb_pair_table(rpb):
    mid = NA_WIN_W - 1
    pad = jnp.zeros(rpb.shape[:2] + (LANES - rpb.shape[2],), rpb.dtype)
    centred = jnp.concatenate([rpb[..., mid:], pad, rpb[..., :mid]], axis=-1)
    shifted = jnp.roll(centred, GRID_W, axis=-1)
    lane = jnp.arange(LANES)
    around_64 = (lane >= GRID_W // 2) & (lane < LANES - GRID_W // 2)
    pair = jnp.where(around_64, shifted[:, 1:], centred[:, :-1])
    return jnp.concatenate([pair, jnp.zeros((rpb.shape[0], 2, LANES), rpb.dtype)], axis=1)


def _na_build_bias(pair_ref, bias_scr):
    c = lax.broadcasted_iota(jnp.int32, (GRID_W, LANES), 0)
    lane = lax.broadcasted_iota(jnp.int32, (GRID_W, LANES), 1)
    kc = lane & (GRID_W - 1)
    ws = jnp.clip(c - NA_WIN_W // 2, 0, GRID_W - NA_WIN_W)
    in_cols = (kc >= ws) & (kc < ws + NA_WIN_W)
    first = lane < GRID_W
    masks = {(True, True): in_cols,
             (True, False): in_cols & first,
             (False, True): in_cols & jnp.logical_not(first)}
    toeplitz = {}
    for kind, i in enumerate(NA_KIND_BLOCKS):
        for a in range(NA_QROWS):
            for grp in range(NA_KROWS // 2):
                v0, d0 = _na_row_geometry(i, a, 2 * grp)
                v1, _ = _na_row_geometry(i, a, 2 * grp + 1)
                if v0 or v1:
                    assert 0 <= d0 <= 2 * NA_WIN_H - 3
                    if d0 not in toeplitz:
                        base = jnp.broadcast_to(pair_ref[0, d0:d0 + 1, :], (GRID_W, LANES))
                        toeplitz[d0] = pltpu.roll(base, 0, 1, stride=1, stride_axis=0) * LOG2E
                    tile = jnp.where(masks[(v0, v1)], toeplitz[d0], NEG_BIG)
                else:
                    tile = jnp.full((GRID_W, LANES), NEG_BIG, F32)
                bias_scr[kind, a * GRID_W:(a + 1) * GRID_W, grp * LANES:(grp + 1) * LANES] = tile


def _na_kernel(q_ref, k_ref, v_ref, g_ref, pair_ref, o_ref, bias_ref, v1_scr):
    d = NA_HEAD_DIM

    @pl.when(pl.program_id(1) == 0)
    def _():
        _na_build_bias(pair_ref, bias_ref)
        v1_scr[:, d:] = jnp.ones((TOK, d), BF16)

    v1_scr[:, :d] = v_ref[0]
    kc = k_ref[0, 0:CTX_LEN, :]
    vc = v1_scr[0:CTX_LEN, :]

    def finish(o1, r0, rows):
        gate = g_ref[0, r0:r0 + rows, :].astype(F32)
        o_ref[0, r0:r0 + rows, :] = (gate * (o1[:, :d] / o1[:, d:])).astype(BF16)

    def scores(i):
        if i is None:
            return None, _dot_nt(q_ref[0, 0:CTX_LEN, :], kc)
        kind = 0 if i == 0 else (2 if i == NA_NBLK - 1 else 1)
        r0 = CTX_LEN + i * NA_QBLK
        k0 = CTX_LEN + _na_key_start(i) * GRID_W
        q = q_ref[0, r0:r0 + NA_QBLK, :]
        return _dot_nt(q, k_ref[0, k0:k0 + NA_KBLK, :]) + bias_ref[kind], _dot_nt(q, kc)

    def attend(i, s_loc, s_ctx):
        m = jnp.max(s_ctx, axis=-1, keepdims=True)
        if i is None:
            finish(_dot(jnp.exp2(s_ctx - m).astype(BF16), vc), 0, CTX_LEN)
            return
        r0 = CTX_LEN + i * NA_QBLK
        k0 = CTX_LEN + _na_key_start(i) * GRID_W
        m = jnp.maximum(jnp.max(s_loc, axis=-1, keepdims=True), m)
        p_loc = jnp.exp2(s_loc - m).astype(BF16)
        p_ctx = jnp.exp2(s_ctx - m).astype(BF16)
        finish(_dot(p_loc, v1_scr[k0:k0 + NA_KBLK, :]) + _dot(p_ctx, vc), r0, NA_QBLK)

    order = list(range(NA_NBLK))
    order.insert(NA_NBLK // 2, None)
    pending = [scores(i) for i in order[:NA_LOOKAHEAD]]
    for n, i in enumerate(order):
        s_loc, s_ctx = pending.pop(0)
        if n + NA_LOOKAHEAD < len(order):
            pending.append(scores(order[n + NA_LOOKAHEAD]))
        attend(i, s_loc, s_ctx)


def _na_core(qkv, g, rpb_pairs):
    nh = NA_HEADS
    return pl.pallas_call(
        _na_kernel,
        grid=(NA_HEADS, BATCH),
        in_specs=[
            pl.BlockSpec((1, TOK, NA_HEAD_DIM), lambda h, b: (b, 0, h)),
            pl.BlockSpec((1, TOK, NA_HEAD_DIM), lambda h, b: (b, 0, nh + h)),
            pl.BlockSpec((1, TOK, NA_HEAD_DIM), lambda h, b: (b, 0, 2 * nh + h)),
            pl.BlockSpec((1, TOK, NA_HEAD_DIM), lambda h, b: (b, 0, h)),
            pl.BlockSpec((1, 2 * NA_WIN_H, LANES), lambda h, b: (h, 0, 0)),
        ],
        out_specs=pl.BlockSpec((1, TOK, NA_HEAD_DIM), lambda h, b: (b, 0, h)),
        out_shape=jax.ShapeDtypeStruct((BATCH, TOK, D_BRANCH), BF16),
        scratch_shapes=[pltpu.VMEM((len(NA_KIND_BLOCKS), NA_QBLK, NA_KBLK), F32),
                        pltpu.VMEM((TOK, 2 * NA_HEAD_DIM), BF16)],
        compiler_params=pltpu.CompilerParams(
            dimension_semantics=("arbitrary", "arbitrary"), vmem_limit_bytes=VMEM_LIMIT),
        name="na_core",
    )(qkv, qkv, qkv, g, rpb_pairs)


def _final_norm_kernel(x_ref, g_ref, o_ref):
    x = x_ref[0]
    o_ref[0] = x * lax.rsqrt(jnp.mean(x * x, axis=-1, keepdims=True) + EPS) * g_ref[...]


def _final_norm(xs, g):
    tm = CTX_LEN
    return pl.pallas_call(
        _final_norm_kernel,
        grid=(BATCH, SEQ // tm),
        in_specs=[
            pl.BlockSpec((1, tm, D_MODEL), lambda b, i: (b, i + CTX_LEN // tm, 0)),
            pl.BlockSpec((1, D_MODEL), lambda b, i: (0, 0)),
        ],
        out_specs=pl.BlockSpec((1, tm, D_MODEL), lambda b, i: (b, i, 0)),
        out_shape=jax.ShapeDtypeStruct((BATCH, SEQ, D_MODEL), F32),
        compiler_params=pltpu.CompilerParams(
            dimension_semantics=("arbitrary", "arbitrary"), vmem_limit_bytes=VMEM_LIMIT),
        name="final_norm",
    )(xs, g)


def _rope_tables():
    quarter = RET_QK_DIM // 4
    freqs = ROPE_BASE ** (-jnp.arange(quarter, dtype=F32) / quarter)
    t = jnp.arange(SEQ)
    ang_r = (t // GRID_W).astype(F32)[:, None] * freqs[None, :]
    ang_c = (t % GRID_W).astype(F32)[:, None] * freqs[None, :]
    cos = jnp.concatenate([jnp.cos(ang_r)] * 2 + [jnp.cos(ang_c)] * 2, axis=-1)
    sin = jnp.concatenate([-jnp.sin(ang_r), jnp.sin(ang_r), -jnp.sin(ang_c), jnp.sin(ang_c)], axis=-1)
    cos = jnp.concatenate([jnp.ones((CTX_LEN, RET_QK_DIM), F32), cos], axis=0)
    sin = jnp.concatenate([jnp.zeros((CTX_LEN, RET_QK_DIM), F32), sin], axis=0)
    return cos, sin


def _per_batch(m):
    ctx_row = jnp.broadcast_to(m[BATCH][None, None, :], (BATCH, 1, m.shape[-1]))
    return jnp.concatenate([ctx_row, m[:BATCH][:, None, :]], axis=1)


def kernel(x, c, ctx, c_ctx, mod_w, mod_b, norm_g, ret_w_in, ret_decay_fwd, ret_decay_bwd, ret_w_out,
           na_w_in, na_rpb, na_w_out, final_g):
    xs = jnp.concatenate([ctx, x], axis=1)
    cvec = jnp.concatenate([c, c_ctx[None], jnp.zeros((8 - BATCH - 1, D_MODEL), F32)], axis=0)
    mods = _modulation(cvec, mod_w, mod_b)
    cos_t, sin_t = _rope_tables()

    for l in range(DEPTH):
        j = l // 2
        sh = _per_batch(mods[l, :, :D_MODEL])
        sc = _per_batch(mods[l, :, D_MODEL:2 * D_MODEL])
        gt = _per_batch(mods[l, :, 2 * D_MODEL:])
        h = _norm_mod(xs, norm_g[l][None, :], sc, sh)
        if l % 2 == 0:
            qk = _in_proj(_in_proj_rope_kernel, "in_proj_ret_qk", h, ret_w_in, j, 0, 2 * D_MODEL, (cos_t, sin_t))
            plain = functools.partial(_in_proj_scaled_kernel, n_scaled=0, scale=1.0)
            v = _in_proj(plain, "in_proj_ret_v", h, ret_w_in, j, 2 * D_MODEL, D_BRANCH)
            g = _in_proj(_in_proj_silu_kernel, "in_proj_ret_g", h, ret_w_in, j, 2 * D_MODEL + D_BRANCH, D_BRANCH)
            raw = jnp.stack([ret_decay_fwd[j], ret_decay_bwd[j]], axis=1)
            raw = jnp.broadcast_to(raw[:, :, None], (RET_HEADS, 2, LANES))
            a = _ret_core(qk, v, g, raw)
            xs = _out_proj(a, ret_w_out, j, xs, gt)
        else:
            scaled_q = functools.partial(_in_proj_scaled_kernel, n_scaled=D_BRANCH // TN_IN,
                                         scale=NA_HEAD_DIM ** -0.5 * LOG2E)
            qkv = _in_proj(scaled_q, "in_proj_na_qkv", h, na_w_in, j, 0, 3 * D_BRANCH)
            g = _in_proj(_in_proj_silu_kernel, "in_proj_na_g", h, na_w_in, j, 3 * D_BRANCH, D_BRANCH)
            a = _na_core(qkv, g, _rpb_pair_table(na_rpb[j]))
            xs = _out_proj(a, na_w_out, j, xs, gt)
    return _final_norm(xs, final_g[None, :])
```

```python
import functools

import numpy as np
import jax
import jax.numpy as jnp
from jax import lax
from jax.experimental import pallas as pl
from jax.experimental.pallas import tpu as pltpu

D_MODEL = 2048
BATCH = 4
SEQ = 2048
DEPTH = 4
GRID_W = 64
GRID_H = SEQ // GRID_W
CTX_LEN = 256
TOK = CTX_LEN + SEQ
D_BRANCH = 2 * D_MODEL
RET_HEADS = 8
RET_QK_DIM = D_MODEL // RET_HEADS
RET_V_DIM = D_BRANCH // RET_HEADS
NA_HEADS = 32
NA_HEAD_DIM = D_BRANCH // NA_HEADS
NA_WIN_H = 8
NA_WIN_W = 16
ROPE_BASE = 10000.0
EPS = 1e-6

F32 = jnp.float32
BF16 = jnp.bfloat16

LANES = 128
VMEM_LIMIT = 56 * 1024 * 1024

TM = TOK // 2
TN_IN = 1024
TM_OUT = TOK // 2
TN_OUT = 512
TM_NORM = TOK // 3
NORM_SLAB = 32
RET_CHUNK = 256
N_CHUNKS = TOK // RET_CHUNK

NA_QROWS = 4
NA_KROWS = 12
NA_QBLK = NA_QROWS * GRID_W
NA_KBLK = NA_KROWS * GRID_W
NA_NBLK = GRID_H // NA_QROWS
NA_LOOKAHEAD = 4
NEG_BIG = -1e30
LOG2E = 1.4426950408889634


def _dot(a, b):
    return jnp.dot(a, b, preferred_element_type=F32)


def _dot_nt(a, b):
    return lax.dot_general(a, b, (((1,), (1,)), ((), ())), preferred_element_type=F32)


def _dot_tn(a, b):
    return lax.dot_general(a, b, (((0,), (0,)), ((), ())), preferred_element_type=F32)


def _silu(x):
    return x * jax.nn.sigmoid(x)


def _is_ctx_rows(tile_idx, rows):
    tok = tile_idx * rows + lax.broadcasted_iota(jnp.int32, (rows, 1), 0)
    return tok < CTX_LEN


def _mod_kernel(c_ref, w_ref, b_ref, o_ref):
    cond = _silu(c_ref[...]).astype(BF16)
    o_ref[0] = _dot(cond, w_ref[0].astype(BF16)) + b_ref[0]


def _modulation(cvec, mod_w, mod_b):
    tn = 1024
    n = 3 * D_MODEL
    return pl.pallas_call(
        _mod_kernel,
        grid=(DEPTH, n // tn),
        in_specs=[
            pl.BlockSpec((8, D_MODEL), lambda l, j: (0, 0)),
            pl.BlockSpec((1, D_MODEL, tn), lambda l, j: (l, 0, j)),
            pl.BlockSpec((1, 1, tn), lambda l, j: (l, 0, j)),
        ],
        out_specs=pl.BlockSpec((1, 8, tn), lambda l, j: (l, 0, j)),
        out_shape=jax.ShapeDtypeStruct((DEPTH, 8, n), F32),
        compiler_params=pltpu.CompilerParams(
            dimension_semantics=("arbitrary", "arbitrary"), vmem_limit_bytes=VMEM_LIMIT),
        name="modulation",
    )(cvec, mod_w, mod_b.reshape(DEPTH, 1, n))


def _norm_mod_kernel(x_ref, g_ref, sc_ref, sh_ref, o_ref):
    ctx_row = jnp.where(pl.program_id(1) == 0, 0, 1)
    g = g_ref[...]
    gain = [g * (1.0 + sc_ref[0, pl.ds(row, 1), :]) for row in (ctx_row, 1)]
    shift = [sh_ref[0, pl.ds(row, 1), :] for row in (ctx_row, 1)]
    for r0 in range(0, TM_NORM, NORM_SLAB):
        which = 0 if r0 < CTX_LEN else 1
        x = x_ref[0, r0:r0 + NORM_SLAB, :]
        y = x * lax.rsqrt(jnp.mean(x * x, axis=-1, keepdims=True) + EPS)
        o_ref[0, r0:r0 + NORM_SLAB, :] = (y * gain[which] + shift[which]).astype(BF16)


def _norm_mod(xs, g, sc, sh):
    tm = TM_NORM
    return pl.pallas_call(
        _norm_mod_kernel,
        grid=(BATCH, TOK // tm),
        in_specs=[
            pl.BlockSpec((1, tm, D_MODEL), lambda b, m: (b, m, 0)),
            pl.BlockSpec((1, D_MODEL), lambda b, m: (0, 0)),
            pl.BlockSpec((1, 2, D_MODEL), lambda b, m: (b, 0, 0)),
            pl.BlockSpec((1, 2, D_MODEL), lambda b, m: (b, 0, 0)),
        ],
        out_specs=pl.BlockSpec((1, tm, D_MODEL), lambda b, m: (b, m, 0)),
        out_shape=jax.ShapeDtypeStruct((BATCH, TOK, D_MODEL), BF16),
        compiler_params=pltpu.CompilerParams(
            dimension_semantics=("arbitrary", "arbitrary"), vmem_limit_bytes=VMEM_LIMIT),
        name="norm_mod",
    )(xs, g, sc, sh)


def _first_token_tile():
    return (pl.program_id(1) == 0) & (pl.program_id(2) == 0)


def _resident_matmul(h_ref, w_ref, wb_scr):
    @pl.when(_first_token_tile())
    def _():
        wb_scr[...] = w_ref[0].astype(BF16)

    return _dot(h_ref[0], wb_scr[...])


def _store_heads(o_ref, c0, val):
    hw = o_ref.shape[-1]
    w = val.shape[-1]
    if w <= hw:
        o_ref[0, c0 // hw, :, c0 % hw:c0 % hw + w] = val
    else:
        for k in range(w // hw):
            o_ref[0, c0 // hw + k] = val[:, k * hw:(k + 1) * hw]


def _in_proj_scaled_kernel(h_ref, w_ref, o_ref, wb_scr, *, n_scaled, scale):
    acc = _resident_matmul(h_ref, w_ref, wb_scr)
    _store_heads(o_ref, 0, (acc * jnp.where(pl.program_id(0) < n_scaled, scale, 1.0)).astype(BF16))


def _in_proj_silu_kernel(h_ref, w_ref, o_ref, wb_scr):
    _store_heads(o_ref, 0, _silu(_resident_matmul(h_ref, w_ref, wb_scr)).astype(BF16))


def _in_proj_rope_kernel(h_ref, w_ref, cos_ref, sin_ref, o_ref, wb_scr):
    acc = _resident_matmul(h_ref, w_ref, wb_scr)
    scale = jnp.where(pl.program_id(0) < D_MODEL // TN_IN, 1.0, RET_QK_DIM ** -0.5)
    for c0 in range(0, TN_IN, LANES):
        t0 = c0 % RET_QK_DIM
        xg = acc[:, c0:c0 + LANES]
        r = xg * cos_ref[:, t0:t0 + LANES] + pltpu.roll(xg, LANES // 2, 1) * sin_ref[:, t0:t0 + LANES]
        _store_heads(o_ref, c0, (r * scale).astype(BF16))


def _in_proj(body, name, h, w_all, layer, col0, ncols, head_w, tables=()):
    tile0 = col0 // TN_IN
    table_spec = pl.BlockSpec((TM, RET_QK_DIM), lambda j, b, m: (m, 0))
    return pl.pallas_call(
        body,
        grid=(ncols // TN_IN, BATCH, TOK // TM),
        in_specs=[
            pl.BlockSpec((1, TM, D_MODEL), lambda j, b, m: (b, m, 0)),
            pl.BlockSpec((1, D_MODEL, TN_IN), lambda j, b, m: (layer, 0, tile0 + j)),
        ] + [table_spec] * len(tables),
        out_specs=pl.BlockSpec((1, TN_IN // head_w, TM, head_w), lambda j, b, m: (b, j, m, 0)),
        out_shape=jax.ShapeDtypeStruct((BATCH, ncols // head_w, TOK, head_w), BF16),
        scratch_shapes=[pltpu.VMEM((D_MODEL, TN_IN), BF16)],
        compiler_params=pltpu.CompilerParams(
            dimension_semantics=("arbitrary", "arbitrary", "arbitrary"), vmem_limit_bytes=VMEM_LIMIT),
        name=name,
    )(h, w_all, *tables)


def _out_proj_kernel(a_ref, w_ref, x_ref, gt_ref, o_ref, wb_scr):
    @pl.when(_first_token_tile())
    def _():
        wb_scr[...] = w_ref[0].astype(BF16)

    a = jnp.concatenate([a_ref[0, hd] for hd in range(a_ref.shape[1])], axis=-1)
    y = _dot(a, wb_scr[...])
    gt = jnp.where(_is_ctx_rows(pl.program_id(2), TM_OUT), gt_ref[0, 0:1, :], gt_ref[0, 1:2, :])
    o_ref[0] = x_ref[0] + gt * y


def _out_proj(a, w_all, layer, xs, gt):
    n_heads, head_w = a.shape[1], a.shape[3]
    return pl.pallas_call(
        _out_proj_kernel,
        grid=(D_MODEL // TN_OUT, BATCH, TOK // TM_OUT),
        in_specs=[
            pl.BlockSpec((1, n_heads, TM_OUT, head_w), lambda j, b, m: (b, 0, m, 0)),
            pl.BlockSpec((1, D_BRANCH, TN_OUT), lambda j, b, m: (layer, 0, j)),
            pl.BlockSpec((1, TM_OUT, TN_OUT), lambda j, b, m: (b, m, j)),
            pl.BlockSpec((1, 2, TN_OUT), lambda j, b, m: (b, 0, j)),
        ],
        out_specs=pl.BlockSpec((1, TM_OUT, TN_OUT), lambda j, b, m: (b, m, j)),
        out_shape=jax.ShapeDtypeStruct((BATCH, TOK, D_MODEL), F32),
        scratch_shapes=[pltpu.VMEM((D_BRANCH, TN_OUT), BF16)],
        input_output_aliases={2: 0},
        compiler_params=pltpu.CompilerParams(
            dimension_semantics=("arbitrary", "arbitrary", "arbitrary"), vmem_limit_bytes=VMEM_LIMIT),
        name="out_proj",
    )(a, w_all, xs, gt)


def _ret_kernel(q_ref, k_ref, v_ref, g_ref, raw_ref, o_ref, of_scr, ob_scr):
    c = RET_CHUNK
    raw = raw_ref[0]
    log_g = -(jnp.maximum(raw, 0.0) + jnp.log1p(jnp.exp(-jnp.abs(raw))))
    pos_i = lax.broadcasted_iota(jnp.int32, (c, 1), 0).astype(F32)
    ii = lax.broadcasted_iota(jnp.int32, (c, c), 0)
    jj = lax.broadcasted_iota(jnp.int32, (c, c), 1)
    diff = (ii - jj).astype(F32)

    def decays(lg, reverse):
        d = -diff if reverse else diff
        intra = jnp.where(d >= 0, jnp.exp(lg * jnp.maximum(d, 0.0)), 0.0)
        p = (c - 1.0 - pos_i) if reverse else pos_i
        q_dec = jnp.exp(lg * (p + 1.0))
        k_dec = jnp.exp(lg * (c - 1.0 - p))
        c_dec = jnp.exp(lg * float(c))
        return intra, q_dec, k_dec, c_dec

    dec_f = decays(log_g[0:1, 0:1], False)
    dec_b = decays(log_g[1:2, 0:1], True)

    def rows(chunk):
        return pl.ds(chunk * c, c)

    def scan_chunk(chunk, s, dec):
        intra, q_dec, k_dec, c_dec = dec
        qc = q_ref[rows(chunk), :]
        kc = k_ref[rows(chunk), :]
        vc = v_ref[rows(chunk), :]
        scores = (_dot_nt(qc, kc) * intra).astype(BF16)
        k_scaled = (kc.astype(F32) * k_dec).astype(BF16)
        if s is None:
            return _dot(scores, vc), _dot_tn(k_scaled, vc)
        q_scaled = (qc.astype(F32) * q_dec).astype(BF16)
        return _dot(scores, vc) + _dot(q_scaled, s.astype(BF16)), s * c_dec + _dot_tn(k_scaled, vc)

    def finalize(chunk, tot):
        nrm = tot * lax.rsqrt(jnp.mean(tot * tot, axis=-1, keepdims=True) + EPS)
        o_ref[0, 0, rows(chunk), :] = (g_ref[rows(chunk), :].astype(F32) * nrm).astype(BF16)

    fwd_order = list(range(N_CHUNKS))
    bwd_order = [0] + list(range(N_CHUNKS - 1, 0, -1))
    s_f = s_b = None
    fwd_seen, bwd_seen = set(), set()
    for cf, cb in zip(fwd_order, bwd_order):
        o_f, s_f = scan_chunk(cf, s_f, dec_f)
        o_b, s_b = scan_chunk(cb, s_b, dec_b)
        fwd_seen.add(cf)
        bwd_seen.add(cb)
        if cf == cb:
            finalize(cf, o_f + o_b)
            continue
        if cf in bwd_seen:
            finalize(cf, o_f + ob_scr[rows(cf), :])
        else:
            of_scr[rows(cf), :] = o_f
        if cb in fwd_seen:
            finalize(cb, of_scr[rows(cb), :] + o_b)
        else:
            ob_scr[rows(cb), :] = o_b


def _ret_core(qk, v, g, raw):
    nh = RET_HEADS
    return pl.pallas_call(
        _ret_kernel,
        grid=(BATCH, RET_HEADS),
        in_specs=[
            pl.BlockSpec((None, None, TOK, RET_QK_DIM), lambda b, h: (b, h, 0, 0)),
            pl.BlockSpec((None, None, TOK, RET_QK_DIM), lambda b, h: (b, nh + h, 0, 0)),
            pl.BlockSpec((None, None, TOK, RET_V_DIM), lambda b, h: (b, h, 0, 0)),
            pl.BlockSpec((None, None, TOK, RET_V_DIM), lambda b, h: (b, h, 0, 0)),
            pl.BlockSpec((1, 2, LANES), lambda b, h: (h, 0, 0)),
        ],
        out_specs=pl.BlockSpec((1, 1, TOK, RET_V_DIM), lambda b, h: (b, h, 0, 0)),
        out_shape=jax.ShapeDtypeStruct((BATCH, RET_HEADS, TOK, RET_V_DIM), BF16),
        scratch_shapes=[pltpu.VMEM((TOK, RET_V_DIM), F32), pltpu.VMEM((TOK, RET_V_DIM), F32)],
        compiler_params=pltpu.CompilerParams(
            dimension_semantics=("arbitrary", "arbitrary"), vmem_limit_bytes=VMEM_LIMIT),
        name="ret_core",
    )(qk, qk, v, g, raw)


def _na_key_start(i):
    return int(np.clip(NA_QROWS * i - NA_WIN_H // 2, 0, GRID_H - NA_KROWS))


def _na_row_geometry(i, a, j):
    r = NA_QROWS * i + a
    rs = int(np.clip(r - NA_WIN_H // 2, 0, GRID_H - NA_WIN_H))
    kr = _na_key_start(i) + j
    return rs <= kr < rs + NA_WIN_H, kr - r + NA_WIN_H - 1


NA_KIND_BLOCKS = (0, 1, NA_NBLK - 1)
for _i in range(2, NA_NBLK - 1):
    assert all(_na_row_geometry(_i, _a, _j) == _na_row_geometry(1, _a, _j)
               for _a in range(NA_QROWS) for _j in range(NA_KROWS))


def _rpb_pair_table(rpb):
    mid = NA_WIN_W - 1
    pad = jnp.zeros(rpb.shape[:2] + (LANES - rpb.shape[2],), rpb.dtype)
    centred = jnp.concatenate([rpb[..., mid:], pad, rpb[..., :mid]], axis=-1)
    shifted = jnp.roll(centred, GRID_W, axis=-1)
    lane = jnp.arange(LANES)
    around_64 = (lane >= GRID_W // 2) & (lane < LANES - GRID_W // 2)
    pair = jnp.where(around_64, shifted[:, 1:], centred[:, :-1])
    return jnp.concatenate([pair, jnp.zeros((rpb.shape[0], 2, LANES), rpb.dtype)], axis=1)


def _na_build_bias(pair_ref, bias_scr):
    c = lax.broadcasted_iota(jnp.int32, (GRID_W, LANES), 0)
    lane = lax.broadcasted_iota(jnp.int32, (GRID_W, LANES), 1)
    kc = lane & (GRID_W - 1)
    ws = jnp.clip(c - NA_WIN_W // 2, 0, GRID_W - NA_WIN_W)
    in_cols = (kc >= ws) & (kc < ws + NA_WIN_W)
    first = lane < GRID_W
    masks = {(True, True): in_cols,
             (True, False): in_cols & first,
             (False, True): in_cols & jnp.logical_not(first)}
    toeplitz = {}
    for kind, i in enumerate(NA_KIND_BLOCKS):
        for a in range(NA_QROWS):
            for grp in range(NA_KROWS // 2):
                v0, d0 = _na_row_geometry(i, a, 2 * grp)
                v1, _ = _na_row_geometry(i, a, 2 * grp + 1)
                if v0 or v1:
                    assert 0 <= d0 <= 2 * NA_WIN_H - 3
                    if d0 not in toeplitz:
                        base = jnp.broadcast_to(pair_ref[0, d0:d0 + 1, :], (GRID_W, LANES))
                        toeplitz[d0] = pltpu.roll(base, 0, 1, stride=1, stride_axis=0) * LOG2E
                    tile = jnp.where(masks[(v0, v1)], toeplitz[d0], NEG_BIG)
                else:
                    tile = jnp.full((GRID_W, LANES), NEG_BIG, F32)
                bias_scr[kind, a * GRID_W:(a + 1) * GRID_W, grp * LANES:(grp + 1) * LANES] = tile


def _na_kernel(q_ref, k_ref, v_ref, g_ref, pair_ref, o_ref, bias_ref, v1_scr):
    d = NA_HEAD_DIM

    @pl.when(pl.program_id(1) == 0)
    def _():
        _na_build_bias(pair_ref, bias_ref)
        v1_scr[:, d:] = jnp.ones((TOK, d), BF16)

    v1_scr[:, :d] = v_ref[...]
    kc = k_ref[0:CTX_LEN, :]
    vc = v1_scr[0:CTX_LEN, :]

    def finish(o1, r0, rows):
        gate = g_ref[r0:r0 + rows, :].astype(F32)
        o_ref[0, 0, r0:r0 + rows, :] = (gate * (o1[:, :d] / o1[:, d:])).astype(BF16)

    def scores(i):
        if i is None:
            return None, _dot_nt(q_ref[0:CTX_LEN, :], kc)
        kind = 0 if i == 0 else (2 if i == NA_NBLK - 1 else 1)
        r0 = CTX_LEN + i * NA_QBLK
        k0 = CTX_LEN + _na_key_start(i) * GRID_W
        q = q_ref[r0:r0 + NA_QBLK, :]
        return _dot_nt(q, k_ref[k0:k0 + NA_KBLK, :]) + bias_ref[kind], _dot_nt(q, kc)

    def attend(i, s_loc, s_ctx):
        m = jnp.max(s_ctx, axis=-1, keepdims=True)
        if i is None:
            finish(_dot(jnp.exp2(s_ctx - m).astype(BF16), vc), 0, CTX_LEN)
            return
        r0 = CTX_LEN + i * NA_QBLK
        k0 = CTX_LEN + _na_key_start(i) * GRID_W
        m = jnp.maximum(jnp.max(s_loc, axis=-1, keepdims=True), m)
        p_loc = jnp.exp2(s_loc - m).astype(BF16)
        p_ctx = jnp.exp2(s_ctx - m).astype(BF16)
        finish(_dot(p_loc, v1_scr[k0:k0 + NA_KBLK, :]) + _dot(p_ctx, vc), r0, NA_QBLK)

    order = list(range(NA_NBLK))
    order.insert(NA_NBLK // 2, None)
    pending = [scores(i) for i in order[:NA_LOOKAHEAD]]
    for n, i in enumerate(order):
        s_loc, s_ctx = pending.pop(0)
        if n + NA_LOOKAHEAD < len(order):
            pending.append(scores(order[n + NA_LOOKAHEAD]))
        attend(i, s_loc, s_ctx)


def _na_core(qkv, g, rpb_pairs):
    nh = NA_HEADS
    head = (None, None, TOK, NA_HEAD_DIM)
    return pl.pallas_call(
        _na_kernel,
        grid=(NA_HEADS, BATCH),
        in_specs=[
            pl.BlockSpec(head, lambda h, b: (b, h, 0, 0)),
            pl.BlockSpec(head, lambda h, b: (b, nh + h, 0, 0)),
            pl.BlockSpec(head, lambda h, b: (b, 2 * nh + h, 0, 0)),
            pl.BlockSpec(head, lambda h, b: (b, h, 0, 0)),
            pl.BlockSpec((1, 2 * NA_WIN_H, LANES), lambda h, b: (h, 0, 0)),
        ],
        out_specs=pl.BlockSpec((1, 1, TOK, NA_HEAD_DIM), lambda h, b: (b, h, 0, 0)),
        out_shape=jax.ShapeDtypeStruct((BATCH, NA_HEADS, TOK, NA_HEAD_DIM), BF16),
        scratch_shapes=[pltpu.VMEM((len(NA_KIND_BLOCKS), NA_QBLK, NA_KBLK), F32),
                        pltpu.VMEM((TOK, 2 * NA_HEAD_DIM), BF16)],
        compiler_params=pltpu.CompilerParams(
            dimension_semantics=("arbitrary", "arbitrary"), vmem_limit_bytes=VMEM_LIMIT),
        name="na_core",
    )(qkv, qkv, qkv, g, rpb_pairs)


def _final_norm_kernel(x_ref, g_ref, o_ref):
    x = x_ref[0]
    o_ref[0] = x * lax.rsqrt(jnp.mean(x * x, axis=-1, keepdims=True) + EPS) * g_ref[...]


def _final_norm(xs, g):
    tm = CTX_LEN
    return pl.pallas_call(
        _final_norm_kernel,
        grid=(BATCH, SEQ // tm),
        in_specs=[
            pl.BlockSpec((1, tm, D_MODEL), lambda b, i: (b, i + CTX_LEN // tm, 0)),
            pl.BlockSpec((1, D_MODEL), lambda b, i: (0, 0)),
        ],
        out_specs=pl.BlockSpec((1, tm, D_MODEL), lambda b, i: (b, i, 0)),
        out_shape=jax.ShapeDtypeStruct((BATCH, SEQ, D_MODEL), F32),
        compiler_params=pltpu.CompilerParams(
            dimension_semantics=("arbitrary", "arbitrary"), vmem_limit_bytes=VMEM_LIMIT),
        name="final_norm",
    )(xs, g)


def _rope_tables():
    quarter = RET_QK_DIM // 4
    freqs = ROPE_BASE ** (-jnp.arange(quarter, dtype=F32) / quarter)
    t = jnp.arange(SEQ)
    ang_r = (t // GRID_W).astype(F32)[:, None] * freqs[None, :]
    ang_c = (t % GRID_W).astype(F32)[:, None] * freqs[None, :]
    cos = jnp.concatenate([jnp.cos(ang_r)] * 2 + [jnp.cos(ang_c)] * 2, axis=-1)
    sin = jnp.concatenate([-jnp.sin(ang_r), jnp.sin(ang_r), -jnp.sin(ang_c), jnp.sin(ang_c)], axis=-1)
    cos = jnp.concatenate([jnp.ones((CTX_LEN, RET_QK_DIM), F32), cos], axis=0)
    sin = jnp.concatenate([jnp.zeros((CTX_LEN, RET_QK_DIM), F32), sin], axis=0)
    return cos, sin


def _per_batch(m):
    ctx_row = jnp.broadcast_to(m[BATCH][None, None, :], (BATCH, 1, m.shape[-1]))
    return jnp.concatenate([ctx_row, m[:BATCH][:, None, :]], axis=1)


def kernel(x, c, ctx, c_ctx, mod_w, mod_b, norm_g, ret_w_in, ret_decay_fwd, ret_decay_bwd, ret_w_out,
           na_w_in, na_rpb, na_w_out, final_g):
    xs = jnp.concatenate([ctx, x], axis=1)
    cvec = jnp.concatenate([c, c_ctx[None], jnp.zeros((8 - BATCH - 1, D_MODEL), F32)], axis=0)
    mods = _modulation(cvec, mod_w, mod_b)
    cos_t, sin_t = _rope_tables()

    for l in range(DEPTH):
        j = l // 2
        sh = _per_batch(mods[l, :, :D_MODEL])
        sc = _per_batch(mods[l, :, D_MODEL:2 * D_MODEL])
        gt = _per_batch(mods[l, :, 2 * D_MODEL:])
        h = _norm_mod(xs, norm_g[l][None, :], sc, sh)
        if l % 2 == 0:
            qk = _in_proj(_in_proj_rope_kernel, "in_proj_ret_qk", h, ret_w_in, j, 0, 2 * D_MODEL, RET_QK_DIM,
                          (cos_t, sin_t))
            plain = functools.partial(_in_proj_scaled_kernel, n_scaled=0, scale=1.0)
            v = _in_proj(plain, "in_proj_ret_v", h, ret_w_in, j, 2 * D_MODEL, D_BRANCH, RET_V_DIM)
            g = _in_proj(_in_proj_silu_kernel, "in_proj_ret_g", h, ret_w_in, j, 2 * D_MODEL + D_BRANCH, D_BRANCH,
                         RET_V_DIM)
            raw = jnp.stack([ret_decay_fwd[j], ret_decay_bwd[j]], axis=1)
            raw = jnp.broadcast_to(raw[:, :, None], (RET_HEADS, 2, LANES))
            a = _ret_core(qk, v, g, raw)
            xs = _out_proj(a, ret_w_out, j, xs, gt)
        else:
            scaled_q = functools.partial(_in_proj_scaled_kernel, n_scaled=D_BRANCH // TN_IN,
                                         scale=NA_HEAD_DIM ** -0.5 * LOG2E)
            qkv = _in_proj(scaled_q, "in_proj_na_qkv", h, na_w_in, j, 0, 3 * D_BRANCH, NA_HEAD_DIM)
            g = _in_proj(_in_proj_silu_kernel, "in_proj_na_g", h, na_w_in, j, 3 * D_BRANCH, D_BRANCH, NA_HEAD_DIM)
            a = _na_core(qkv, g, _rpb_pair_table(na_rpb[j]))
            xs = _out_proj(a, na_w_out, j, xs, gt)
    return _final_norm(xs, final_g[None, :])
```

```python
import functools

import numpy as np
import jax
import jax.numpy as jnp
from jax import lax
from jax.experimental import pallas as pl
from jax.experimental.pallas import tpu as pltpu

D_MODEL = 2048
BATCH = 4
SEQ = 2048
DEPTH = 4
GRID_W = 64
GRID_H = SEQ // GRID_W
CTX_LEN = 256
TOK = CTX_LEN + SEQ
D_BRANCH = 2 * D_MODEL
RET_HEADS = 8
RET_QK_DIM = D_MODEL // RET_HEADS
RET_V_DIM = D_BRANCH // RET_HEADS
NA_HEADS = 32
NA_HEAD_DIM = D_BRANCH // NA_HEADS
NA_WIN_H = 8
NA_WIN_W = 16
ROPE_BASE = 10000.0
EPS = 1e-6

F32 = jnp.float32
BF16 = jnp.bfloat16

LANES = 128
VMEM_LIMIT = 56 * 1024 * 1024

TM = TOK // 2
TN_IN = 1024
MM_ROWS = TM // 4
TM_OUT = TOK // 2
TN_OUT = 512
TM_NORM = TOK // 3
NORM_SLAB = 32
RET_CHUNK = 256
N_CHUNKS = TOK // RET_CHUNK

NA_QROWS = 4
NA_KROWS = 12
NA_QBLK = NA_QROWS * GRID_W
NA_KBLK = NA_KROWS * GRID_W
NA_NBLK = GRID_H // NA_QROWS
NA_HPS = 2
NA_LOOKAHEAD = 4 * NA_HPS
NEG_BIG = -1e30
LOG2E = 1.4426950408889634


def _dot(a, b):
    return jnp.dot(a, b, preferred_element_type=F32)


def _dot_nt(a, b):
    return lax.dot_general(a, b, (((1,), (1,)), ((), ())), preferred_element_type=F32)


def _dot_tn(a, b):
    return lax.dot_general(a, b, (((0,), (0,)), ((), ())), preferred_element_type=F32)


def _silu(x):
    half = 0.5 * x
    return half + half * jnp.tanh(half)


def _mod_kernel(c_ref, w_ref, b_ref, o_ref):
    cond = _silu(c_ref[...]).astype(BF16)
    o_ref[0] = _dot(cond, w_ref[0].astype(BF16)) + b_ref[0]


def _modulation(cvec, mod_w, mod_b):
    tn = 1024
    n = 3 * D_MODEL
    return pl.pallas_call(
        _mod_kernel,
        grid=(DEPTH, n // tn),
        in_specs=[
            pl.BlockSpec((8, D_MODEL), lambda l, j: (0, 0)),
            pl.BlockSpec((1, D_MODEL, tn), lambda l, j: (l, 0, j)),
            pl.BlockSpec((1, 1, tn), lambda l, j: (l, 0, j)),
        ],
        out_specs=pl.BlockSpec((1, 8, tn), lambda l, j: (l, 0, j)),
        out_shape=jax.ShapeDtypeStruct((DEPTH, 8, n), F32),
        compiler_params=pltpu.CompilerParams(
            dimension_semantics=("arbitrary", "arbitrary"), vmem_limit_bytes=VMEM_LIMIT),
        name="modulation",
    )(cvec, mod_w, mod_b.reshape(DEPTH, 1, n))


def _norm_mod_kernel(x_ref, g_ref, sc_ref, sh_ref, o_ref):
    ctx_row = jnp.where(pl.program_id(1) == 0, 0, 1)
    g = g_ref[...]
    gain = [g * (1.0 + sc_ref[0, pl.ds(row, 1), :]) for row in (ctx_row, 1)]
    shift = [sh_ref[0, pl.ds(row, 1), :] for row in (ctx_row, 1)]
    for r0 in range(0, TM_NORM, NORM_SLAB):
        which = 0 if r0 < CTX_LEN else 1
        x = x_ref[0, r0:r0 + NORM_SLAB, :]
        y = x * lax.rsqrt(jnp.mean(x * x, axis=-1, keepdims=True) + EPS)
        o_ref[0, r0:r0 + NORM_SLAB, :] = (y * gain[which] + shift[which]).astype(BF16)


def _norm_mod(xs, g, sc, sh):
    tm = TM_NORM
    return pl.pallas_call(
        _norm_mod_kernel,
        grid=(BATCH, TOK // tm),
        in_specs=[
            pl.BlockSpec((1, tm, D_MODEL), lambda b, m: (b, m, 0)),
            pl.BlockSpec((1, D_MODEL), lambda b, m: (0, 0)),
            pl.BlockSpec((1, 2, D_MODEL), lambda b, m: (b, 0, 0)),
            pl.BlockSpec((1, 2, D_MODEL), lambda b, m: (b, 0, 0)),
        ],
        out_specs=pl.BlockSpec((1, tm, D_MODEL), lambda b, m: (b, m, 0)),
        out_shape=jax.ShapeDtypeStruct((BATCH, TOK, D_MODEL), BF16),
        compiler_params=pltpu.CompilerParams(
            dimension_semantics=("arbitrary", "arbitrary"), vmem_limit_bytes=VMEM_LIMIT),
        name="norm_mod",
    )(xs, g, sc, sh)


def _first_token_tile():
    return (pl.program_id(1) == 0) & (pl.program_id(2) == 0)


def _resident_matmul(h_ref, w_ref, wb_scr):
    @pl.when(_first_token_tile())
    def _():
        wb_scr[...] = w_ref[0].astype(BF16)

    return [(r0, _dot(h_ref[0, r0:r0 + MM_ROWS, :], wb_scr[...])) for r0 in range(0, TM, MM_ROWS)]


def _store_heads(o_ref, r0, c0, val):
    hw = o_ref.shape[-1]
    rows, w = val.shape
    if w <= hw:
        o_ref[0, c0 // hw, r0:r0 + rows, c0 % hw:c0 % hw + w] = val
    else:
        for k in range(w // hw):
            o_ref[0, c0 // hw + k, r0:r0 + rows, :] = val[:, k * hw:(k + 1) * hw]


def _in_proj_scaled_kernel(h_ref, w_ref, o_ref, wb_scr, *, n_scaled, scale):
    factor = jnp.where(pl.program_id(0) < n_scaled, scale, 1.0)
    for r0, acc in _resident_matmul(h_ref, w_ref, wb_scr):
        _store_heads(o_ref, r0, 0, (acc * factor).astype(BF16))


def _in_proj_silu_kernel(h_ref, w_ref, o_ref, wb_scr):
    for r0, acc in _resident_matmul(h_ref, w_ref, wb_scr):
        _store_heads(o_ref, r0, 0, _silu(acc).astype(BF16))


def _in_proj_rope_kernel(h_ref, w_ref, cos_ref, sin_ref, o_ref, wb_scr):
    scale = jnp.where(pl.program_id(0) < D_MODEL // TN_IN, 1.0, RET_QK_DIM ** -0.5)
    for r0, acc in _resident_matmul(h_ref, w_ref, wb_scr):
        rows = slice(r0, r0 + MM_ROWS)
        for c0 in range(0, TN_IN, LANES):
            t0 = c0 % RET_QK_DIM
            xg = acc[:, c0:c0 + LANES]
            r = xg * cos_ref[rows, t0:t0 + LANES] + pltpu.roll(xg, LANES // 2, 1) * sin_ref[rows, t0:t0 + LANES]
            _store_heads(o_ref, r0, c0, (r * scale).astype(BF16))


def _in_proj(body, name, h, w_all, layer, col0, ncols, head_w, tables=()):
    tile0 = col0 // TN_IN
    table_spec = pl.BlockSpec((TM, RET_QK_DIM), lambda j, b, m: (m, 0))
    return pl.pallas_call(
        body,
        grid=(ncols // TN_IN, BATCH, TOK // TM),
        in_specs=[
            pl.BlockSpec((1, TM, D_MODEL), lambda j, b, m: (b, m, 0)),
            pl.BlockSpec((1, D_MODEL, TN_IN), lambda j, b, m: (layer, 0, tile0 + j)),
        ] + [table_spec] * len(tables),
        out_specs=pl.BlockSpec((1, TN_IN // head_w, TM, head_w), lambda j, b, m: (b, j, m, 0)),
        out_shape=jax.ShapeDtypeStruct((BATCH, ncols // head_w, TOK, head_w), BF16),
        scratch_shapes=[pltpu.VMEM((D_MODEL, TN_IN), BF16)],
        compiler_params=pltpu.CompilerParams(
            dimension_semantics=("arbitrary", "arbitrary", "arbitrary"), vmem_limit_bytes=VMEM_LIMIT),
        name=name,
    )(h, w_all, *tables)


def _out_proj_kernel(a_ref, w_ref, x_ref, gt_ref, o_ref, wb_scr):
    @pl.when(_first_token_tile())
    def _():
        wb_scr[...] = w_ref[0].astype(BF16)

    for r0 in range(0, TM_OUT, MM_ROWS):
        rows = slice(r0, r0 + MM_ROWS)
        a = jnp.concatenate([a_ref[0, hd, rows, :] for hd in range(a_ref.shape[1])], axis=-1)
        y = _dot(a, wb_scr[...])
        tok = pl.program_id(2) * TM_OUT + r0 + lax.broadcasted_iota(jnp.int32, (MM_ROWS, 1), 0)
        gt = jnp.where(tok < CTX_LEN, gt_ref[0, 0:1, :], gt_ref[0, 1:2, :])
        o_ref[0, rows, :] = x_ref[0, rows, :] + gt * y


def _out_proj(a, w_all, layer, xs, gt):
    n_heads, head_w = a.shape[1], a.shape[3]
    return pl.pallas_call(
        _out_proj_kernel,
        grid=(D_MODEL // TN_OUT, BATCH, TOK // TM_OUT),
        in_specs=[
            pl.BlockSpec((1, n_heads, TM_OUT, head_w), lambda j, b, m: (b, 0, m, 0)),
            pl.BlockSpec((1, D_BRANCH, TN_OUT), lambda j, b, m: (layer, 0, j)),
            pl.BlockSpec((1, TM_OUT, TN_OUT), lambda j, b, m: (b, m, j)),
            pl.BlockSpec((1, 2, TN_OUT), lambda j, b, m: (b, 0, j)),
        ],
        out_specs=pl.BlockSpec((1, TM_OUT, TN_OUT), lambda j, b, m: (b, m, j)),
        out_shape=jax.ShapeDtypeStruct((BATCH, TOK, D_MODEL), F32),
        scratch_shapes=[pltpu.VMEM((D_BRANCH, TN_OUT), BF16)],
        input_output_aliases={2: 0},
        compiler_params=pltpu.CompilerParams(
            dimension_semantics=("arbitrary", "arbitrary", "arbitrary"), vmem_limit_bytes=VMEM_LIMIT),
        name="out_proj",
    )(a, w_all, xs, gt)


def _ret_kernel(q_ref, k_ref, v_ref, g_ref, raw_ref, o_ref, of_scr, ob_scr):
    c = RET_CHUNK
    raw = raw_ref[0]
    log_g = -(jnp.maximum(raw, 0.0) + jnp.log1p(jnp.exp(-jnp.abs(raw))))
    pos_i = lax.broadcasted_iota(jnp.int32, (c, 1), 0).astype(F32)
    ii = lax.broadcasted_iota(jnp.int32, (c, c), 0)
    jj = lax.broadcasted_iota(jnp.int32, (c, c), 1)
    diff = (ii - jj).astype(F32)

    def decays(lg, reverse):
        d = -diff if reverse else diff
        intra = jnp.where(d >= 0, jnp.exp(lg * jnp.maximum(d, 0.0)), 0.0)
        p = (c - 1.0 - pos_i) if reverse else pos_i
        q_dec = jnp.exp(lg * (p + 1.0))
        k_dec = jnp.exp(lg * (c - 1.0 - p))
        c_dec = jnp.exp(lg * float(c))
        return intra, q_dec, k_dec, c_dec

    dec_f = decays(log_g[0:1, 0:1], False)
    dec_b = decays(log_g[1:2, 0:1], True)

    def rows(chunk):
        return pl.ds(chunk * c, c)

    def scan_chunk(chunk, s, dec):
        intra, q_dec, k_dec, c_dec = dec
        qc = q_ref[rows(chunk), :]
        kc = k_ref[rows(chunk), :]
        vc = v_ref[rows(chunk), :]
        scores = (_dot_nt(qc, kc) * intra).astype(BF16)
        k_scaled = (kc.astype(F32) * k_dec).astype(BF16)
        if s is None:
            return _dot(scores, vc), _dot_tn(k_scaled, vc)
        q_scaled = (qc.astype(F32) * q_dec).astype(BF16)
        return _dot(scores, vc) + _dot(q_scaled, s.astype(BF16)), s * c_dec + _dot_tn(k_scaled, vc)

    def finalize(chunk, tot):
        nrm = tot * lax.rsqrt(jnp.mean(tot * tot, axis=-1, keepdims=True) + EPS)
        o_ref[0, 0, rows(chunk), :] = (g_ref[rows(chunk), :].astype(F32) * nrm).astype(BF16)

    fwd_order = list(range(N_CHUNKS))
    bwd_order = [0] + list(range(N_CHUNKS - 1, 0, -1))
    s_f = s_b = None
    fwd_seen, bwd_seen = set(), set()
    for cf, cb in zip(fwd_order, bwd_order):
        o_f, s_f = scan_chunk(cf, s_f, dec_f)
        o_b, s_b = scan_chunk(cb, s_b, dec_b)
        fwd_seen.add(cf)
        bwd_seen.add(cb)
        if cf == cb:
            finalize(cf, o_f + o_b)
            continue
        if cf in bwd_seen:
            finalize(cf, o_f + ob_scr[rows(cf), :])
        else:
            of_scr[rows(cf), :] = o_f
        if cb in fwd_seen:
            finalize(cb, of_scr[rows(cb), :] + o_b)
        else:
            ob_scr[rows(cb), :] = o_b


def _ret_core(qk, v, g, raw):
    nh = RET_HEADS
    return pl.pallas_call(
        _ret_kernel,
        grid=(BATCH, RET_HEADS),
        in_specs=[
            pl.BlockSpec((None, None, TOK, RET_QK_DIM), lambda b, h: (b, h, 0, 0)),
            pl.BlockSpec((None, None, TOK, RET_QK_DIM), lambda b, h: (b, nh + h, 0, 0)),
            pl.BlockSpec((None, None, TOK, RET_V_DIM), lambda b, h: (b, h, 0, 0)),
            pl.BlockSpec((None, None, TOK, RET_V_DIM), lambda b, h: (b, h, 0, 0)),
            pl.BlockSpec((1, 2, LANES), lambda b, h: (h, 0, 0)),
        ],
        out_specs=pl.BlockSpec((1, 1, TOK, RET_V_DIM), lambda b, h: (b, h, 0, 0)),
        out_shape=jax.ShapeDtypeStruct((BATCH, RET_HEADS, TOK, RET_V_DIM), BF16),
        scratch_shapes=[pltpu.VMEM((TOK, RET_V_DIM), F32), pltpu.VMEM((TOK, RET_V_DIM), F32)],
        compiler_params=pltpu.CompilerParams(
            dimension_semantics=("arbitrary", "arbitrary"), vmem_limit_bytes=VMEM_LIMIT),
        name="ret_core",
    )(qk, qk, v, g, raw)


def _na_key_start(i):
    return int(np.clip(NA_QROWS * i - NA_WIN_H // 2, 0, GRID_H - NA_KROWS))


def _na_row_geometry(i, a, j):
    r = NA_QROWS * i + a
    rs = int(np.clip(r - NA_WIN_H // 2, 0, GRID_H - NA_WIN_H))
    kr = _na_key_start(i) + j
    return rs <= kr < rs + NA_WIN_H, kr - r + NA_WIN_H - 1


NA_KIND_BLOCKS = (0, 1, NA_NBLK - 1)
for _i in range(2, NA_NBLK - 1):
    assert all(_na_row_geometry(_i, _a, _j) == _na_row_geometry(1, _a, _j)
               for _a in range(NA_QROWS) for _j in range(NA_KROWS))


def _rpb_pair_table(rpb):
    mid = NA_WIN_W - 1
    pad = jnp.zeros(rpb.shape[:2] + (LANES - rpb.shape[2],), rpb.dtype)
    centred = jnp.concatenate([rpb[..., mid:], pad, rpb[..., :mid]], axis=-1)
    shifted = jnp.roll(centred, GRID_W, axis=-1)
    lane = jnp.arange(LANES)
    around_64 = (lane >= GRID_W // 2) & (lane < LANES - GRID_W // 2)
    pair = jnp.where(around_64, shifted[:, 1:], centred[:, :-1])
    return jnp.concatenate([pair, jnp.zeros((rpb.shape[0], 2, LANES), rpb.dtype)], axis=1)


def _na_build_bias(pair_ref, bias_scr):
    c = lax.broadcasted_iota(jnp.int32, (GRID_W, LANES), 0)
    lane = lax.broadcasted_iota(jnp.int32, (GRID_W, LANES), 1)
    kc = lane & (GRID_W - 1)
    ws = jnp.clip(c - NA_WIN_W // 2, 0, GRID_W - NA_WIN_W)
    in_cols = (kc >= ws) & (kc < ws + NA_WIN_W)
    first = lane < GRID_W
    masks = {(True, True): in_cols,
             (True, False): in_cols & first,
             (False, True): in_cols & jnp.logical_not(first)}
    toeplitz = {}
    for kind, i in enumerate(NA_KIND_BLOCKS):
        for a in range(NA_QROWS):
            for grp in range(NA_KROWS // 2):
                v0, d0 = _na_row_geometry(i, a, 2 * grp)
                v1, _ = _na_row_geometry(i, a, 2 * grp + 1)
                if v0 or v1:
                    assert 0 <= d0 <= 2 * NA_WIN_H - 3
                    if d0 not in toeplitz:
                        base = jnp.broadcast_to(pair_ref[d0:d0 + 1, :], (GRID_W, LANES))
                        toeplitz[d0] = pltpu.roll(base, 0, 1, stride=1, stride_axis=0) * LOG2E
                    tile = jnp.where(masks[(v0, v1)], toeplitz[d0], NEG_BIG)
                else:
                    tile = jnp.full((GRID_W, LANES), NEG_BIG, F32)
                bias_scr[kind, a * GRID_W:(a + 1) * GRID_W, grp * LANES:(grp + 1) * LANES] = tile


def _na_kernel(q_ref, k_ref, v_ref, g_ref, pair_ref, o_ref, bias_ref, v1_scr):
    d = NA_HEAD_DIM

    @pl.when(pl.program_id(1) == 0)
    def _():
        for hd in range(NA_HPS):
            _na_build_bias(pair_ref.at[hd], bias_ref.at[hd])
            v1_scr[hd, :, d:] = jnp.ones((TOK, d), BF16)

    for hd in range(NA_HPS):
        v1_scr[hd, :, :d] = v_ref[hd]

    def finish(hd, o1, r0, rows):
        gate = g_ref[hd, r0:r0 + rows, :].astype(F32)
        o_ref[0, hd, r0:r0 + rows, :] = (gate * (o1[:, :d] / o1[:, d:])).astype(BF16)

    def scores(job):
        hd, i = job
        kc = k_ref[hd, 0:CTX_LEN, :]
        if i is None:
            return None, _dot_nt(q_ref[hd, 0:CTX_LEN, :], kc)
        kind = 0 if i == 0 else (2 if i == NA_NBLK - 1 else 1)
        r0 = CTX_LEN + i * NA_QBLK
        k0 = CTX_LEN + _na_key_start(i) * GRID_W
        q = q_ref[hd, r0:r0 + NA_QBLK, :]
        return _dot_nt(q, k_ref[hd, k0:k0 + NA_KBLK, :]) + bias_ref[hd, kind], _dot_nt(q, kc)

    def attend(job, s_loc, s_ctx):
        hd, i = job
        vc = v1_scr[hd, 0:CTX_LEN, :]
        m = jnp.max(s_ctx, axis=-1, keepdims=True)
        if i is None:
            finish(hd, _dot(jnp.exp2(s_ctx - m).astype(BF16), vc), 0, CTX_LEN)
            return
        r0 = CTX_LEN + i * NA_QBLK
        k0 = CTX_LEN + _na_key_start(i) * GRID_W
        m = jnp.maximum(jnp.max(s_loc, axis=-1, keepdims=True), m)
        p_loc = jnp.exp2(s_loc - m).astype(BF16)
        p_ctx = jnp.exp2(s_ctx - m).astype(BF16)
        finish(hd, _dot(p_loc, v1_scr[hd, k0:k0 + NA_KBLK, :]) + _dot(p_ctx, vc), r0, NA_QBLK)

    blocks = list(range(NA_NBLK))
    blocks.insert(NA_NBLK // 2, None)
    order = [(hd, i) for i in blocks for hd in range(NA_HPS)]
    pending = [scores(job) for job in order[:NA_LOOKAHEAD]]
    for n, job in enumerate(order):
        s_loc, s_ctx = pending.pop(0)
        if n + NA_LOOKAHEAD < len(order):
            pending.append(scores(order[n + NA_LOOKAHEAD]))
        attend(job, s_loc, s_ctx)


def _na_core(qkv, g, rpb_pairs):
    nblk = NA_HEADS // NA_HPS
    heads = (None, NA_HPS, TOK, NA_HEAD_DIM)
    return pl.pallas_call(
        _na_kernel,
        grid=(nblk, BATCH),
        in_specs=[
            pl.BlockSpec(heads, lambda h, b: (b, h, 0, 0)),
            pl.BlockSpec(heads, lambda h, b: (b, nblk + h, 0, 0)),
            pl.BlockSpec(heads, lambda h, b: (b, 2 * nblk + h, 0, 0)),
            pl.BlockSpec(heads, lambda h, b: (b, h, 0, 0)),
            pl.BlockSpec((NA_HPS, 2 * NA_WIN_H, LANES), lambda h, b: (h, 0, 0)),
        ],
        out_specs=pl.BlockSpec((1, NA_HPS, TOK, NA_HEAD_DIM), lambda h, b: (b, h, 0, 0)),
        out_shape=jax.ShapeDtypeStruct((BATCH, NA_HEADS, TOK, NA_HEAD_DIM), BF16),
        scratch_shapes=[pltpu.VMEM((NA_HPS, len(NA_KIND_BLOCKS), NA_QBLK, NA_KBLK), F32),
                        pltpu.VMEM((NA_HPS, TOK, 2 * NA_HEAD_DIM), BF16)],
        compiler_params=pltpu.CompilerParams(
            dimension_semantics=("arbitrary", "arbitrary"), vmem_limit_bytes=VMEM_LIMIT),
        name="na_core",
    )(qkv, qkv, qkv, g, rpb_pairs)


def _final_norm_kernel(x_ref, g_ref, o_ref):
    x = x_ref[0]
    o_ref[0] = x * lax.rsqrt(jnp.mean(x * x, axis=-1, keepdims=True) + EPS) * g_ref[...]


def _final_norm(xs, g):
    tm = CTX_LEN
    return pl.pallas_call(
        _final_norm_kernel,
        grid=(BATCH, SEQ // tm),
        in_specs=[
            pl.BlockSpec((1, tm, D_MODEL), lambda b, i: (b, i + CTX_LEN // tm, 0)),
            pl.BlockSpec((1, D_MODEL), lambda b, i: (0, 0)),
        ],
        out_specs=pl.BlockSpec((1, tm, D_MODEL), lambda b, i: (b, i, 0)),
        out_shape=jax.ShapeDtypeStruct((BATCH, SEQ, D_MODEL), F32),
        compiler_params=pltpu.CompilerParams(
            dimension_semantics=("arbitrary", "arbitrary"), vmem_limit_bytes=VMEM_LIMIT),
        name="final_norm",
    )(xs, g)


def _rope_tables():
    quarter = RET_QK_DIM // 4
    freqs = ROPE_BASE ** (-jnp.arange(quarter, dtype=F32) / quarter)
    t = jnp.arange(SEQ)
    ang_r = (t // GRID_W).astype(F32)[:, None] * freqs[None, :]
    ang_c = (t % GRID_W).astype(F32)[:, None] * freqs[None, :]
    cos = jnp.concatenate([jnp.cos(ang_r)] * 2 + [jnp.cos(ang_c)] * 2, axis=-1)
    sin = jnp.concatenate([-jnp.sin(ang_r), jnp.sin(ang_r), -jnp.sin(ang_c), jnp.sin(ang_c)], axis=-1)
    cos = jnp.concatenate([jnp.ones((CTX_LEN, RET_QK_DIM), F32), cos], axis=0)
    sin = jnp.concatenate([jnp.zeros((CTX_LEN, RET_QK_DIM), F32), sin], axis=0)
    return cos, sin


def _per_batch(m):
    ctx_row = jnp.broadcast_to(m[BATCH][None, None, :], (BATCH, 1, m.shape[-1]))
    return jnp.concatenate([ctx_row, m[:BATCH][:, None, :]], axis=1)


def kernel(x, c, ctx, c_ctx, mod_w, mod_b, norm_g, ret_w_in, ret_decay_fwd, ret_decay_bwd, ret_w_out,
           na_w_in, na_rpb, na_w_out, final_g):
    xs = jnp.concatenate([ctx, x], axis=1)
    cvec = jnp.concatenate([c, c_ctx[None], jnp.zeros((8 - BATCH - 1, D_MODEL), F32)], axis=0)
    mods = _modulation(cvec, mod_w, mod_b)
    cos_t, sin_t = _rope_tables()

    for l in range(DEPTH):
        j = l // 2
        sh = _per_batch(mods[l, :, :D_MODEL])
        sc = _per_batch(mods[l, :, D_MODEL:2 * D_MODEL])
        gt = _per_batch(mods[l, :, 2 * D_MODEL:])
        h = _norm_mod(xs, norm_g[l][None, :], sc, sh)
        if l % 2 == 0:
            qk = _in_proj(_in_proj_rope_kernel, "in_proj_ret_qk", h, ret_w_in, j, 0, 2 * D_MODEL, RET_QK_DIM,
                          (cos_t, sin_t))
            plain = functools.partial(_in_proj_scaled_kernel, n_scaled=0, scale=1.0)
            v = _in_proj(plain, "in_proj_ret_v", h, ret_w_in, j, 2 * D_MODEL, D_BRANCH, RET_V_DIM)
            g = _in_proj(_in_proj_silu_kernel, "in_proj_ret_g", h, ret_w_in, j, 2 * D_MODEL + D_BRANCH, D_BRANCH,
                         RET_V_DIM)
            raw = jnp.stack([ret_decay_fwd[j], ret_decay_bwd[j]], axis=1)
            raw = jnp.broadcast_to(raw[:, :, None], (RET_HEADS, 2, LANES))
            a = _ret_core(qk, v, g, raw)
            xs = _out_proj(a, ret_w_out, j, xs, gt)
        else:
            scaled_q = functools.partial(_in_proj_scaled_kernel, n_scaled=D_BRANCH // TN_IN,
                                         scale=NA_HEAD_DIM ** -0.5 * LOG2E)
            qkv = _in_proj(scaled_q, "in_proj_na_qkv", h, na_w_in, j, 0, 3 * D_BRANCH, NA_HEAD_DIM)
            g = _in_proj(_in_proj_silu_kernel, "in_proj_na_g", h, na_w_in, j, 3 * D_BRANCH, D_BRANCH, NA_HEAD_DIM)
            a = _na_core(qkv, g, _rpb_pair_table(na_rpb[j]))
            xs = _out_proj(a, na_w_out, j, xs, gt)
    return _final_norm(xs, final_g[None, :])
```

```python
import functools

import numpy as np
import jax
import jax.numpy as jnp
from jax import lax
from jax.experimental import pallas as pl
from jax.experimental.pallas import tpu as pltpu

D_MODEL = 2048
BATCH = 4
SEQ = 2048
DEPTH = 4
GRID_W = 64
GRID_H = SEQ // GRID_W
CTX_LEN = 256
TOK = CTX_LEN + SEQ
D_BRANCH = 2 * D_MODEL
RET_HEADS = 8
RET_QK_DIM = D_MODEL // RET_HEADS
RET_V_DIM = D_BRANCH // RET_HEADS
NA_HEADS = 32
NA_HEAD_DIM = D_BRANCH // NA_HEADS
NA_WIN_H = 8
NA_WIN_W = 16
ROPE_BASE = 10000.0
EPS = 1e-6

F32 = jnp.float32
BF16 = jnp.bfloat16

LANES = 128
VMEM_LIMIT = 56 * 1024 * 1024

TM = TOK
TM_ROPE = TOK // 2
TN_IN = 1024
MM_ROWS = TOK // 8
TM_OUT = TOK // 2
TN_OUT = 512
TM_NORM = TOK // 3
NORM_SLAB = 32
RET_CHUNK = 256
N_CHUNKS = TOK // RET_CHUNK

NA_QROWS = 4
NA_KROWS = 12
NA_QBLK = NA_QROWS * GRID_W
NA_KBLK = NA_KROWS * GRID_W
NA_NBLK = GRID_H // NA_QROWS
NA_HPS = 2
NA_LOOKAHEAD = 4 * NA_HPS
NEG_BIG = -1e30
LOG2E = 1.4426950408889634


def _dot(a, b):
    return jnp.dot(a, b, preferred_element_type=F32)


def _dot_nt(a, b):
    return lax.dot_general(a, b, (((1,), (1,)), ((), ())), preferred_element_type=F32)


def _dot_tn(a, b):
    return lax.dot_general(a, b, (((0,), (0,)), ((), ())), preferred_element_type=F32)


def _silu(x):
    half = 0.5 * x
    return half + half * jnp.tanh(half)


def _mod_kernel(c_ref, w_ref, b_ref, o_ref):
    cond = _silu(c_ref[...]).astype(BF16)
    o_ref[0] = _dot(cond, w_ref[0].astype(BF16)) + b_ref[0]


def _modulation(cvec, mod_w, mod_b):
    tn = 1024
    n = 3 * D_MODEL
    return pl.pallas_call(
        _mod_kernel,
        grid=(DEPTH, n // tn),
        in_specs=[
            pl.BlockSpec((8, D_MODEL), lambda l, j: (0, 0)),
            pl.BlockSpec((1, D_MODEL, tn), lambda l, j: (l, 0, j)),
            pl.BlockSpec((1, 1, tn), lambda l, j: (l, 0, j)),
        ],
        out_specs=pl.BlockSpec((1, 8, tn), lambda l, j: (l, 0, j)),
        out_shape=jax.ShapeDtypeStruct((DEPTH, 8, n), F32),
        compiler_params=pltpu.CompilerParams(
            dimension_semantics=("arbitrary", "arbitrary"), vmem_limit_bytes=VMEM_LIMIT),
        name="modulation",
    )(cvec, mod_w, mod_b.reshape(DEPTH, 1, n))


def _norm_mod_kernel(x_ref, g_ref, sc_ref, sh_ref, o_ref):
    ctx_row = jnp.where(pl.program_id(1) == 0, 0, 1)
    g = g_ref[...]
    gain = [g * (1.0 + sc_ref[0, pl.ds(row, 1), :]) for row in (ctx_row, 1)]
    shift = [sh_ref[0, pl.ds(row, 1), :] for row in (ctx_row, 1)]
    for r0 in range(0, TM_NORM, NORM_SLAB):
        which = 0 if r0 < CTX_LEN else 1
        x = x_ref[0, r0:r0 + NORM_SLAB, :]
        y = x * lax.rsqrt(jnp.mean(x * x, axis=-1, keepdims=True) + EPS)
        o_ref[0, r0:r0 + NORM_SLAB, :] = (y * gain[which] + shift[which]).astype(BF16)


def _norm_mod(xs, g, sc, sh):
    tm = TM_NORM
    return pl.pallas_call(
        _norm_mod_kernel,
        grid=(BATCH, TOK // tm),
        in_specs=[
            pl.BlockSpec((1, tm, D_MODEL), lambda b, m: (b, m, 0)),
            pl.BlockSpec((1, D_MODEL), lambda b, m: (0, 0)),
            pl.BlockSpec((1, 2, D_MODEL), lambda b, m: (b, 0, 0)),
            pl.BlockSpec((1, 2, D_MODEL), lambda b, m: (b, 0, 0)),
        ],
        out_specs=pl.BlockSpec((1, tm, D_MODEL), lambda b, m: (b, m, 0)),
        out_shape=jax.ShapeDtypeStruct((BATCH, TOK, D_MODEL), BF16),
        compiler_params=pltpu.CompilerParams(
            dimension_semantics=("arbitrary", "arbitrary"), vmem_limit_bytes=VMEM_LIMIT),
        name="norm_mod",
    )(xs, g, sc, sh)


def _first_token_tile():
    return (pl.program_id(1) == 0) & (pl.program_id(2) == 0)


def _resident_matmul(h_ref, w_ref, wb_scr):
    @pl.when(_first_token_tile())
    def _():
        wb_scr[...] = w_ref[0].astype(BF16)

    return [(r0, _dot(h_ref[0, r0:r0 + MM_ROWS, :], wb_scr[...])) for r0 in range(0, h_ref.shape[1], MM_ROWS)]


def _store_heads(o_ref, r0, c0, val):
    hw = o_ref.shape[-1]
    rows, w = val.shape
    if w <= hw:
        o_ref[0, c0 // hw, r0:r0 + rows, c0 % hw:c0 % hw + w] = val
    else:
        for k in range(w // hw):
            o_ref[0, c0 // hw + k, r0:r0 + rows, :] = val[:, k * hw:(k + 1) * hw]


def _in_proj_scaled_kernel(h_ref, w_ref, o_ref, wb_scr, *, n_scaled, scale):
    factor = jnp.where(pl.program_id(0) < n_scaled, scale, 1.0)
    for r0, acc in _resident_matmul(h_ref, w_ref, wb_scr):
        _store_heads(o_ref, r0, 0, (acc * factor).astype(BF16))


def _in_proj_silu_kernel(h_ref, w_ref, o_ref, wb_scr):
    for r0, acc in _resident_matmul(h_ref, w_ref, wb_scr):
        _store_heads(o_ref, r0, 0, _silu(acc).astype(BF16))


def _in_proj_rope_kernel(h_ref, w_ref, cos_ref, sin_ref, o_ref, wb_scr):
    scale = jnp.where(pl.program_id(0) < D_MODEL // TN_IN, 1.0, RET_QK_DIM ** -0.5)
    for r0, acc in _resident_matmul(h_ref, w_ref, wb_scr):
        rows = slice(r0, r0 + MM_ROWS)
        for c0 in range(0, TN_IN, LANES):
            t0 = c0 % RET_QK_DIM
            xg = acc[:, c0:c0 + LANES]
            r = xg * cos_ref[rows, t0:t0 + LANES] + pltpu.roll(xg, LANES // 2, 1) * sin_ref[rows, t0:t0 + LANES]
            _store_heads(o_ref, r0, c0, (r * scale).astype(BF16))


def _in_proj(body, name, h, w_all, layer, col0, ncols, head_w, tables=()):
    tile0 = col0 // TN_IN
    tm = TM_ROPE if tables else TM
    table_spec = pl.BlockSpec((tm, RET_QK_DIM), lambda j, b, m: (m, 0))
    return pl.pallas_call(
        body,
        grid=(ncols // TN_IN, BATCH, TOK // tm),
        in_specs=[
            pl.BlockSpec((1, tm, D_MODEL), lambda j, b, m: (b, m, 0)),
            pl.BlockSpec((1, D_MODEL, TN_IN), lambda j, b, m: (layer, 0, tile0 + j)),
        ] + [table_spec] * len(tables),
        out_specs=pl.BlockSpec((1, TN_IN // head_w, tm, head_w), lambda j, b, m: (b, j, m, 0)),
        out_shape=jax.ShapeDtypeStruct((BATCH, ncols // head_w, TOK, head_w), BF16),
        scratch_shapes=[pltpu.VMEM((D_MODEL, TN_IN), BF16)],
        compiler_params=pltpu.CompilerParams(
            dimension_semantics=("arbitrary", "arbitrary", "arbitrary"), vmem_limit_bytes=VMEM_LIMIT),
        name=name,
    )(h, w_all, *tables)


def _out_proj_kernel(a_ref, w_ref, x_ref, gt_ref, o_ref, wb_scr):
    @pl.when(_first_token_tile())
    def _():
        wb_scr[...] = w_ref[0].astype(BF16)

    for r0 in range(0, TM_OUT, MM_ROWS):
        rows = slice(r0, r0 + MM_ROWS)
        a = jnp.concatenate([a_ref[0, hd, rows, :] for hd in range(a_ref.shape[1])], axis=-1)
        y = _dot(a, wb_scr[...])
        tok = pl.program_id(2) * TM_OUT + r0 + lax.broadcasted_iota(jnp.int32, (MM_ROWS, 1), 0)
        gt = jnp.where(tok < CTX_LEN, gt_ref[0, 0:1, :], gt_ref[0, 1:2, :])
        o_ref[0, rows, :] = x_ref[0, rows, :] + gt * y


def _out_proj(a, w_all, layer, xs, gt):
    n_heads, head_w = a.shape[1], a.shape[3]
    return pl.pallas_call(
        _out_proj_kernel,
        grid=(D_MODEL // TN_OUT, BATCH, TOK // TM_OUT),
        in_specs=[
            pl.BlockSpec((1, n_heads, TM_OUT, head_w), lambda j, b, m: (b, 0, m, 0)),
            pl.BlockSpec((1, D_BRANCH, TN_OUT), lambda j, b, m: (layer, 0, j)),
            pl.BlockSpec((1, TM_OUT, TN_OUT), lambda j, b, m: (b, m, j)),
            pl.BlockSpec((1, 2, TN_OUT), lambda j, b, m: (b, 0, j)),
        ],
        out_specs=pl.BlockSpec((1, TM_OUT, TN_OUT), lambda j, b, m: (b, m, j)),
        out_shape=jax.ShapeDtypeStruct((BATCH, TOK, D_MODEL), F32),
        scratch_shapes=[pltpu.VMEM((D_BRANCH, TN_OUT), BF16)],
        input_output_aliases={2: 0},
        compiler_params=pltpu.CompilerParams(
            dimension_semantics=("arbitrary", "arbitrary", "arbitrary"), vmem_limit_bytes=VMEM_LIMIT),
        name="out_proj",
    )(a, w_all, xs, gt)


def _ret_kernel(q_ref, k_ref, v_ref, g_ref, raw_ref, o_ref, of_scr, ob_scr):
    c = RET_CHUNK
    raw = raw_ref[0]
    log_g = -(jnp.maximum(raw, 0.0) + jnp.log1p(jnp.exp(-jnp.abs(raw))))
    pos_i = lax.broadcasted_iota(jnp.int32, (c, 1), 0).astype(F32)
    ii = lax.broadcasted_iota(jnp.int32, (c, c), 0)
    jj = lax.broadcasted_iota(jnp.int32, (c, c), 1)
    diff = (ii - jj).astype(F32)

    def decays(lg, reverse):
        d = -diff if reverse else diff
        intra = jnp.where(d >= 0, jnp.exp(lg * jnp.maximum(d, 0.0)), 0.0)
        p = (c - 1.0 - pos_i) if reverse else pos_i
        q_dec = jnp.exp(lg * (p + 1.0))
        k_dec = jnp.exp(lg * (c - 1.0 - p))
        c_dec = jnp.exp(lg * float(c))
        return intra, q_dec, k_dec, c_dec

    dec_f = decays(log_g[0:1, 0:1], False)
    dec_b = decays(log_g[1:2, 0:1], True)

    def rows(chunk):
        return pl.ds(chunk * c, c)

    def scan_chunk(chunk, s, dec):
        intra, q_dec, k_dec, c_dec = dec
        qc = q_ref[rows(chunk), :]
        kc = k_ref[rows(chunk), :]
        vc = v_ref[rows(chunk), :]
        scores = (_dot_nt(qc, kc) * intra).astype(BF16)
        k_scaled = (kc.astype(F32) * k_dec).astype(BF16)
        if s is None:
            return _dot(scores, vc), _dot_tn(k_scaled, vc)
        q_scaled = (qc.astype(F32) * q_dec).astype(BF16)
        return _dot(scores, vc) + _dot(q_scaled, s.astype(BF16)), s * c_dec + _dot_tn(k_scaled, vc)

    def finalize(chunk, tot):
        nrm = tot * lax.rsqrt(jnp.mean(tot * tot, axis=-1, keepdims=True) + EPS)
        o_ref[0, 0, rows(chunk), :] = (g_ref[rows(chunk), :].astype(F32) * nrm).astype(BF16)

    fwd_order = list(range(N_CHUNKS))
    bwd_order = [0] + list(range(N_CHUNKS - 1, 0, -1))
    s_f = s_b = None
    fwd_seen, bwd_seen = set(), set()
    for cf, cb in zip(fwd_order, bwd_order):
        o_f, s_f = scan_chunk(cf, s_f, dec_f)
        o_b, s_b = scan_chunk(cb, s_b, dec_b)
        fwd_seen.add(cf)
        bwd_seen.add(cb)
        if cf == cb:
            finalize(cf, o_f + o_b)
            continue
        if cf in bwd_seen:
            finalize(cf, o_f + ob_scr[rows(cf), :])
        else:
            of_scr[rows(cf), :] = o_f
        if cb in fwd_seen:
            finalize(cb, of_scr[rows(cb), :] + o_b)
        else:
            ob_scr[rows(cb), :] = o_b


def _ret_core(qk, v, g, raw):
    nh = RET_HEADS
    return pl.pallas_call(
        _ret_kernel,
        grid=(BATCH, RET_HEADS),
        in_specs=[
            pl.BlockSpec((None, None, TOK, RET_QK_DIM), lambda b, h: (b, h, 0, 0)),
            pl.BlockSpec((None, None, TOK, RET_QK_DIM), lambda b, h: (b, nh + h, 0, 0)),
            pl.BlockSpec((None, None, TOK, RET_V_DIM), lambda b, h: (b, h, 0, 0)),
            pl.BlockSpec((None, None, TOK, RET_V_DIM), lambda b, h: (b, h, 0, 0)),
            pl.BlockSpec((1, 2, LANES), lambda b, h: (h, 0, 0)),
        ],
        out_specs=pl.BlockSpec((1, 1, TOK, RET_V_DIM), lambda b, h: (b, h, 0, 0)),
        out_shape=jax.ShapeDtypeStruct((BATCH, RET_HEADS, TOK, RET_V_DIM), BF16),
        scratch_shapes=[pltpu.VMEM((TOK, RET_V_DIM), F32), pltpu.VMEM((TOK, RET_V_DIM), F32)],
        compiler_params=pltpu.CompilerParams(
            dimension_semantics=("arbitrary", "arbitrary"), vmem_limit_bytes=VMEM_LIMIT),
        name="ret_core",
    )(qk, qk, v, g, raw)


def _na_key_start(i):
    return int(np.clip(NA_QROWS * i - NA_WIN_H // 2, 0, GRID_H - NA_KROWS))


def _na_row_geometry(i, a, j):
    r = NA_QROWS * i + a
    rs = int(np.clip(r - NA_WIN_H // 2, 0, GRID_H - NA_WIN_H))
    kr = _na_key_start(i) + j
    return rs <= kr < rs + NA_WIN_H, kr - r + NA_WIN_H - 1


NA_KIND_BLOCKS = (0, 1, NA_NBLK - 1)
for _i in range(2, NA_NBLK - 1):
    assert all(_na_row_geometry(_i, _a, _j) == _na_row_geometry(1, _a, _j)
               for _a in range(NA_QROWS) for _j in range(NA_KROWS))


def _rpb_pair_table(rpb):
    mid = NA_WIN_W - 1
    pad = jnp.zeros(rpb.shape[:2] + (LANES - rpb.shape[2],), rpb.dtype)
    centred = jnp.concatenate([rpb[..., mid:], pad, rpb[..., :mid]], axis=-1)
    shifted = jnp.roll(centred, GRID_W, axis=-1)
    lane = jnp.arange(LANES)
    around_64 = (lane >= GRID_W // 2) & (lane < LANES - GRID_W // 2)
    pair = jnp.where(around_64, shifted[:, 1:], centred[:, :-1])
    return jnp.concatenate([pair, jnp.zeros((rpb.shape[0], 2, LANES), rpb.dtype)], axis=1)


def _na_build_bias(pair_ref, bias_scr):
    c = lax.broadcasted_iota(jnp.int32, (GRID_W, LANES), 0)
    lane = lax.broadcasted_iota(jnp.int32, (GRID_W, LANES), 1)
    kc = lane & (GRID_W - 1)
    ws = jnp.clip(c - NA_WIN_W // 2, 0, GRID_W - NA_WIN_W)
    in_cols = (kc >= ws) & (kc < ws + NA_WIN_W)
    first = lane < GRID_W
    masks = {(True, True): in_cols,
             (True, False): in_cols & first,
             (False, True): in_cols & jnp.logical_not(first)}
    toeplitz = {}
    for kind, i in enumerate(NA_KIND_BLOCKS):
        for a in range(NA_QROWS):
            for grp in range(NA_KROWS // 2):
                v0, d0 = _na_row_geometry(i, a, 2 * grp)
                v1, _ = _na_row_geometry(i, a, 2 * grp + 1)
                if v0 or v1:
                    assert 0 <= d0 <= 2 * NA_WIN_H - 3
                    if d0 not in toeplitz:
                        base = jnp.broadcast_to(pair_ref[d0:d0 + 1, :], (GRID_W, LANES))
                        toeplitz[d0] = pltpu.roll(base, 0, 1, stride=1, stride_axis=0) * LOG2E
                    tile = jnp.where(masks[(v0, v1)], toeplitz[d0], NEG_BIG)
                else:
                    tile = jnp.full((GRID_W, LANES), NEG_BIG, F32)
                bias_scr[kind, a * GRID_W:(a + 1) * GRID_W, grp * LANES:(grp + 1) * LANES] = tile


def _na_kernel(q_ref, k_ref, v_ref, g_ref, pair_ref, o_ref, bias_ref, v1_scr):
    d = NA_HEAD_DIM

    @pl.when(pl.program_id(1) == 0)
    def _():
        for hd in range(NA_HPS):
            _na_build_bias(pair_ref.at[hd], bias_ref.at[hd])
            v1_scr[hd, :, d:] = jnp.ones((TOK, d), BF16)

    for hd in range(NA_HPS):
        v1_scr[hd, :, :d] = v_ref[hd]

    def finish(hd, o1, r0, rows):
        gate = g_ref[hd, r0:r0 + rows, :].astype(F32)
        o_ref[0, hd, r0:r0 + rows, :] = (gate * (o1[:, :d] / o1[:, d:])).astype(BF16)

    def scores(job):
        hd, i = job
        kc = k_ref[hd, 0:CTX_LEN, :]
        if i is None:
            return None, _dot_nt(q_ref[hd, 0:CTX_LEN, :], kc)
        kind = 0 if i == 0 else (2 if i == NA_NBLK - 1 else 1)
        r0 = CTX_LEN + i * NA_QBLK
        k0 = CTX_LEN + _na_key_start(i) * GRID_W
        q = q_ref[hd, r0:r0 + NA_QBLK, :]
        return _dot_nt(q, k_ref[hd, k0:k0 + NA_KBLK, :]) + bias_ref[hd, kind], _dot_nt(q, kc)

    def attend(job, s_loc, s_ctx):
        hd, i = job
        vc = v1_scr[hd, 0:CTX_LEN, :]
        m = jnp.max(s_ctx, axis=-1, keepdims=True)
        if i is None:
            finish(hd, _dot(jnp.exp2(s_ctx - m).astype(BF16), vc), 0, CTX_LEN)
            return
        r0 = CTX_LEN + i * NA_QBLK
        k0 = CTX_LEN + _na_key_start(i) * GRID_W
        m = jnp.maximum(jnp.max(s_loc, axis=-1, keepdims=True), m)
        p_loc = jnp.exp2(s_loc - m).astype(BF16)
        p_ctx = jnp.exp2(s_ctx - m).astype(BF16)
        finish(hd, _dot(p_loc, v1_scr[hd, k0:k0 + NA_KBLK, :]) + _dot(p_ctx, vc), r0, NA_QBLK)

    blocks = list(range(NA_NBLK))
    blocks.insert(NA_NBLK // 2, None)
    order = [(hd, i) for i in blocks for hd in range(NA_HPS)]
    pending = [scores(job) for job in order[:NA_LOOKAHEAD]]
    for n, job in enumerate(order):
        s_loc, s_ctx = pending.pop(0)
        if n + NA_LOOKAHEAD < len(order):
            pending.append(scores(order[n + NA_LOOKAHEAD]))
        attend(job, s_loc, s_ctx)


def _na_core(qkv, g, rpb_pairs):
    nblk = NA_HEADS // NA_HPS
    heads = (None, NA_HPS, TOK, NA_HEAD_DIM)
    return pl.pallas_call(
        _na_kernel,
        grid=(nblk, BATCH),
        in_specs=[
            pl.BlockSpec(heads, lambda h, b: (b, h, 0, 0)),
            pl.BlockSpec(heads, lambda h, b: (b, nblk + h, 0, 0)),
            pl.BlockSpec(heads, lambda h, b: (b, 2 * nblk + h, 0, 0)),
            pl.BlockSpec(heads, lambda h, b: (b, h, 0, 0)),
            pl.BlockSpec((NA_HPS, 2 * NA_WIN_H, LANES), lambda h, b: (h, 0, 0)),
        ],
        out_specs=pl.BlockSpec((1, NA_HPS, TOK, NA_HEAD_DIM), lambda h, b: (b, h, 0, 0)),
        out_shape=jax.ShapeDtypeStruct((BATCH, NA_HEADS, TOK, NA_HEAD_DIM), BF16),
        scratch_shapes=[pltpu.VMEM((NA_HPS, len(NA_KIND_BLOCKS), NA_QBLK, NA_KBLK), F32),
                        pltpu.VMEM((NA_HPS, TOK, 2 * NA_HEAD_DIM), BF16)],
        compiler_params=pltpu.CompilerParams(
            dimension_semantics=("arbitrary", "arbitrary"), vmem_limit_bytes=VMEM_LIMIT),
        name="na_core",
    )(qkv, qkv, qkv, g, rpb_pairs)


def _final_norm_kernel(x_ref, g_ref, o_ref):
    x = x_ref[0]
    o_ref[0] = x * lax.rsqrt(jnp.mean(x * x, axis=-1, keepdims=True) + EPS) * g_ref[...]


def _final_norm(xs, g):
    tm = CTX_LEN
    return pl.pallas_call(
        _final_norm_kernel,
        grid=(BATCH, SEQ // tm),
        in_specs=[
            pl.BlockSpec((1, tm, D_MODEL), lambda b, i: (b, i + CTX_LEN // tm, 0)),
            pl.BlockSpec((1, D_MODEL), lambda b, i: (0, 0)),
        ],
        out_specs=pl.BlockSpec((1, tm, D_MODEL), lambda b, i: (b, i, 0)),
        out_shape=jax.ShapeDtypeStruct((BATCH, SEQ, D_MODEL), F32),
        compiler_params=pltpu.CompilerParams(
            dimension_semantics=("arbitrary", "arbitrary"), vmem_limit_bytes=VMEM_LIMIT),
        name="final_norm",
    )(xs, g)


def _rope_tables():
    quarter = RET_QK_DIM // 4
    freqs = ROPE_BASE ** (-jnp.arange(quarter, dtype=F32) / quarter)
    t = jnp.arange(SEQ)
    ang_r = (t // GRID_W).astype(F32)[:, None] * freqs[None, :]
    ang_c = (t % GRID_W).astype(F32)[:, None] * freqs[None, :]
    cos = jnp.concatenate([jnp.cos(ang_r)] * 2 + [jnp.cos(ang_c)] * 2, axis=-1)
    sin = jnp.concatenate([-jnp.sin(ang_r), jnp.sin(ang_r), -jnp.sin(ang_c), jnp.sin(ang_c)], axis=-1)
    cos = jnp.concatenate([jnp.ones((CTX_LEN, RET_QK_DIM), F32), cos], axis=0)
    sin = jnp.concatenate([jnp.zeros((CTX_LEN, RET_QK_DIM), F32), sin], axis=0)
    return cos, sin


def _per_batch(m):
    ctx_row = jnp.broadcast_to(m[BATCH][None, None, :], (BATCH, 1, m.shape[-1]))
    return jnp.concatenate([ctx_row, m[:BATCH][:, None, :]], axis=1)


def kernel(x, c, ctx, c_ctx, mod_w, mod_b, norm_g, ret_w_in, ret_decay_fwd, ret_decay_bwd, ret_w_out,
           na_w_in, na_rpb, na_w_out, final_g):
    xs = jnp.concatenate([ctx, x], axis=1)
    cvec = jnp.concatenate([c, c_ctx[None], jnp.zeros((8 - BATCH - 1, D_MODEL), F32)], axis=0)
    mods = _modulation(cvec, mod_w, mod_b)
    cos_t, sin_t = _rope_tables()

    for l in range(DEPTH):
        j = l // 2
        sh = _per_batch(mods[l, :, :D_MODEL])
        sc = _per_batch(mods[l, :, D_MODEL:2 * D_MODEL])
        gt = _per_batch(mods[l, :, 2 * D_MODEL:])
        h = _norm_mod(xs, norm_g[l][None, :], sc, sh)
        if l % 2 == 0:
            qk = _in_proj(_in_proj_rope_kernel, "in_proj_ret_qk", h, ret_w_in, j, 0, 2 * D_MODEL, RET_QK_DIM,
                          (cos_t, sin_t))
            plain = functools.partial(_in_proj_scaled_kernel, n_scaled=0, scale=1.0)
            v = _in_proj(plain, "in_proj_ret_v", h, ret_w_in, j, 2 * D_MODEL, D_BRANCH, RET_V_DIM)
            g = _in_proj(_in_proj_silu_kernel, "in_proj_ret_g", h, ret_w_in, j, 2 * D_MODEL + D_BRANCH, D_BRANCH,
                         RET_V_DIM)
            raw = jnp.stack([ret_decay_fwd[j], ret_decay_bwd[j]], axis=1)
            raw = jnp.broadcast_to(raw[:, :, None], (RET_HEADS, 2, LANES))
            a = _ret_core(qk, v, g, raw)
            xs = _out_proj(a, ret_w_out, j, xs, gt)
        else:
            scaled_q = functools.partial(_in_proj_scaled_kernel, n_scaled=D_BRANCH // TN_IN,
                                         scale=NA_HEAD_DIM ** -0.5 * LOG2E)
            qkv = _in_proj(scaled_q, "in_proj_na_qkv", h, na_w_in, j, 0, 3 * D_BRANCH, NA_HEAD_DIM)
            g = _in_proj(_in_proj_silu_kernel, "in_proj_na_g", h, na_w_in, j, 3 * D_BRANCH, D_BRANCH, NA_HEAD_DIM)
            a = _na_core(qkv, g, _rpb_pair_table(na_rpb[j]))
            xs = _out_proj(a, na_w_out, j, xs, gt)
    return _final_norm(xs, final_g[None, :])
```

```python
import functools

import numpy as np
import jax
import jax.numpy as jnp
from jax import lax
from jax.experimental import pallas as pl
from jax.experimental.pallas import tpu as pltpu

D_MODEL = 2048
BATCH = 4
SEQ = 2048
DEPTH = 4
GRID_W = 64
GRID_H = SEQ // GRID_W
CTX_LEN = 256
TOK = CTX_LEN + SEQ
D_BRANCH = 2 * D_MODEL
RET_HEADS = 8
RET_QK_DIM = D_MODEL // RET_HEADS
RET_V_DIM = D_BRANCH // RET_HEADS
NA_HEADS = 32
NA_HEAD_DIM = D_BRANCH // NA_HEADS
NA_WIN_H = 8
NA_WIN_W = 16
ROPE_BASE = 10000.0
EPS = 1e-6

F32 = jnp.float32
BF16 = jnp.bfloat16

LANES = 128
VMEM_LIMIT = 58 * 1024 * 1024

TM = TOK
TN_IN = 1024
MM_ROWS = TOK // 8
TM_OUT = TOK // 2
TN_OUT = 512
TM_NORM = TOK // 3
NORM_SLAB = 32
RET_CHUNK = 256
N_CHUNKS = TOK // RET_CHUNK

NA_QROWS = 4
NA_KROWS = 12
NA_QBLK = NA_QROWS * GRID_W
NA_KBLK = NA_KROWS * GRID_W
NA_NBLK = GRID_H // NA_QROWS
NA_HPS = 4
NA_LOOKAHEAD = 4 * NA_HPS
NEG_BIG = -1e30
LOG2E = 1.4426950408889634


def _dot(a, b):
    return jnp.dot(a, b, preferred_element_type=F32)


def _dot_nt(a, b):
    return lax.dot_general(a, b, (((1,), (1,)), ((), ())), preferred_element_type=F32)


def _dot_tn(a, b):
    return lax.dot_general(a, b, (((0,), (0,)), ((), ())), preferred_element_type=F32)


def _silu(x):
    half = 0.5 * x
    return half + half * jnp.tanh(half)


def _mod_kernel(c_ref, w_ref, b_ref, o_ref):
    cond = _silu(c_ref[...]).astype(BF16)
    o_ref[0] = _dot(cond, w_ref[0].astype(BF16)) + b_ref[0]


def _modulation(cvec, mod_w, mod_b):
    tn = 1024
    n = 3 * D_MODEL
    return pl.pallas_call(
        _mod_kernel,
        grid=(DEPTH, n // tn),
        in_specs=[
            pl.BlockSpec((8, D_MODEL), lambda l, j: (0, 0)),
            pl.BlockSpec((1, D_MODEL, tn), lambda l, j: (l, 0, j)),
            pl.BlockSpec((1, 1, tn), lambda l, j: (l, 0, j)),
        ],
        out_specs=pl.BlockSpec((1, 8, tn), lambda l, j: (l, 0, j)),
        out_shape=jax.ShapeDtypeStruct((DEPTH, 8, n), F32),
        compiler_params=pltpu.CompilerParams(
            dimension_semantics=("arbitrary", "arbitrary"), vmem_limit_bytes=VMEM_LIMIT),
        name="modulation",
    )(cvec, mod_w, mod_b.reshape(DEPTH, 1, n))


def _norm_mod_kernel(x_ref, g_ref, sc_ref, sh_ref, o_ref):
    ctx_row = jnp.where(pl.program_id(1) == 0, 0, 1)
    g = g_ref[...]
    gain = [g * (1.0 + sc_ref[0, pl.ds(row, 1), :]) for row in (ctx_row, 1)]
    shift = [sh_ref[0, pl.ds(row, 1), :] for row in (ctx_row, 1)]
    for r0 in range(0, TM_NORM, NORM_SLAB):
        which = 0 if r0 < CTX_LEN else 1
        x = x_ref[0, r0:r0 + NORM_SLAB, :]
        y = x * lax.rsqrt(jnp.mean(x * x, axis=-1, keepdims=True) + EPS)
        o_ref[0, r0:r0 + NORM_SLAB, :] = (y * gain[which] + shift[which]).astype(BF16)


def _norm_mod(xs, g, sc, sh):
    tm = TM_NORM
    return pl.pallas_call(
        _norm_mod_kernel,
        grid=(BATCH, TOK // tm),
        in_specs=[
            pl.BlockSpec((1, tm, D_MODEL), lambda b, m: (b, m, 0)),
            pl.BlockSpec((1, D_MODEL), lambda b, m: (0, 0)),
            pl.BlockSpec((1, 2, D_MODEL), lambda b, m: (b, 0, 0)),
            pl.BlockSpec((1, 2, D_MODEL), lambda b, m: (b, 0, 0)),
        ],
        out_specs=pl.BlockSpec((1, tm, D_MODEL), lambda b, m: (b, m, 0)),
        out_shape=jax.ShapeDtypeStruct((BATCH, TOK, D_MODEL), BF16),
        compiler_params=pltpu.CompilerParams(
            dimension_semantics=("arbitrary", "arbitrary"), vmem_limit_bytes=VMEM_LIMIT),
        name="norm_mod",
    )(xs, g, sc, sh)


def _first_token_tile():
    return (pl.program_id(1) == 0) & (pl.program_id(2) == 0)


def _resident_matmul(h_ref, w_ref, wb_scr):
    @pl.when(_first_token_tile())
    def _():
        wb_scr[...] = w_ref[0].astype(BF16)

    return [(r0, _dot(h_ref[0, r0:r0 + MM_ROWS, :], wb_scr[...])) for r0 in range(0, h_ref.shape[1], MM_ROWS)]


def _store_heads(o_ref, r0, c0, val):
    hw = o_ref.shape[-1]
    rows, w = val.shape
    if w <= hw:
        o_ref[0, c0 // hw, r0:r0 + rows, c0 % hw:c0 % hw + w] = val
    else:
        for k in range(w // hw):
            o_ref[0, c0 // hw + k, r0:r0 + rows, :] = val[:, k * hw:(k + 1) * hw]


def _in_proj_scaled_kernel(h_ref, w_ref, o_ref, wb_scr, *, n_scaled, scale):
    factor = jnp.where(pl.program_id(0) < n_scaled, scale, 1.0)
    for r0, acc in _resident_matmul(h_ref, w_ref, wb_scr):
        _store_heads(o_ref, r0, 0, (acc * factor).astype(BF16))


def _in_proj_silu_kernel(h_ref, w_ref, o_ref, wb_scr):
    for r0, acc in _resident_matmul(h_ref, w_ref, wb_scr):
        _store_heads(o_ref, r0, 0, _silu(acc).astype(BF16))


def _in_proj_rope_kernel(h_ref, w_ref, cos_ref, sin_ref, o_ref, wb_scr):
    scale = jnp.where(pl.program_id(0) < D_MODEL // TN_IN, 1.0, RET_QK_DIM ** -0.5)
    for r0, acc in _resident_matmul(h_ref, w_ref, wb_scr):
        rows = slice(r0, r0 + MM_ROWS)
        for c0 in range(0, TN_IN, LANES):
            t0 = c0 % RET_QK_DIM
            xg = acc[:, c0:c0 + LANES]
            r = xg * cos_ref[rows, t0:t0 + LANES] + pltpu.roll(xg, LANES // 2, 1) * sin_ref[rows, t0:t0 + LANES]
            _store_heads(o_ref, r0, c0, (r * scale).astype(BF16))


def _in_proj(body, name, h, w_all, layer, col0, ncols, head_w, tables=()):
    tile0 = col0 // TN_IN
    tm = TM
    table_spec = pl.BlockSpec((tm, RET_QK_DIM), lambda j, b, m: (m, 0), pipeline_mode=pl.Buffered(1))
    return pl.pallas_call(
        body,
        grid=(ncols // TN_IN, BATCH, TOK // tm),
        in_specs=[
            pl.BlockSpec((1, tm, D_MODEL), lambda j, b, m: (b, m, 0)),
            pl.BlockSpec((1, D_MODEL, TN_IN), lambda j, b, m: (layer, 0, tile0 + j)),
        ] + [table_spec] * len(tables),
        out_specs=pl.BlockSpec((1, TN_IN // head_w, tm, head_w), lambda j, b, m: (b, j, m, 0)),
        out_shape=jax.ShapeDtypeStruct((BATCH, ncols // head_w, TOK, head_w), BF16),
        scratch_shapes=[pltpu.VMEM((D_MODEL, TN_IN), BF16)],
        compiler_params=pltpu.CompilerParams(
            dimension_semantics=("arbitrary", "arbitrary", "arbitrary"), vmem_limit_bytes=VMEM_LIMIT),
        name=name,
    )(h, w_all, *tables)


def _out_proj_kernel(a_ref, w_ref, x_ref, gt_ref, o_ref, wb_scr):
    @pl.when(_first_token_tile())
    def _():
        wb_scr[...] = w_ref[0].astype(BF16)

    for r0 in range(0, TM_OUT, MM_ROWS):
        rows = slice(r0, r0 + MM_ROWS)
        a = jnp.concatenate([a_ref[0, hd, rows, :] for hd in range(a_ref.shape[1])], axis=-1)
        y = _dot(a, wb_scr[...])
        tok = pl.program_id(2) * TM_OUT + r0 + lax.broadcasted_iota(jnp.int32, (MM_ROWS, 1), 0)
        gt = jnp.where(tok < CTX_LEN, gt_ref[0, 0:1, :], gt_ref[0, 1:2, :])
        o_ref[0, rows, :] = x_ref[0, rows, :] + gt * y


def _out_proj(a, w_all, layer, xs, gt):
    n_heads, head_w = a.shape[1], a.shape[3]
    return pl.pallas_call(
        _out_proj_kernel,
        grid=(D_MODEL // TN_OUT, BATCH, TOK // TM_OUT),
        in_specs=[
            pl.BlockSpec((1, n_heads, TM_OUT, head_w), lambda j, b, m: (b, 0, m, 0)),
            pl.BlockSpec((1, D_BRANCH, TN_OUT), lambda j, b, m: (layer, 0, j)),
            pl.BlockSpec((1, TM_OUT, TN_OUT), lambda j, b, m: (b, m, j)),
            pl.BlockSpec((1, 2, TN_OUT), lambda j, b, m: (b, 0, j)),
        ],
        out_specs=pl.BlockSpec((1, TM_OUT, TN_OUT), lambda j, b, m: (b, m, j)),
        out_shape=jax.ShapeDtypeStruct((BATCH, TOK, D_MODEL), F32),
        scratch_shapes=[pltpu.VMEM((D_BRANCH, TN_OUT), BF16)],
        input_output_aliases={2: 0},
        compiler_params=pltpu.CompilerParams(
            dimension_semantics=("arbitrary", "arbitrary", "arbitrary"), vmem_limit_bytes=VMEM_LIMIT),
        name="out_proj",
    )(a, w_all, xs, gt)


def _ret_kernel(q_ref, k_ref, v_ref, g_ref, raw_ref, o_ref, of_scr, ob_scr):
    c = RET_CHUNK
    raw = raw_ref[0]
    log_g = -(jnp.maximum(raw, 0.0) + jnp.log1p(jnp.exp(-jnp.abs(raw))))
    pos_i = lax.broadcasted_iota(jnp.int32, (c, 1), 0).astype(F32)
    ii = lax.broadcasted_iota(jnp.int32, (c, c), 0)
    jj = lax.broadcasted_iota(jnp.int32, (c, c), 1)
    diff = (ii - jj).astype(F32)

    def decays(lg, reverse):
        d = -diff if reverse else diff
        intra = jnp.where(d >= 0, jnp.exp(lg * jnp.maximum(d, 0.0)), 0.0)
        p = (c - 1.0 - pos_i) if reverse else pos_i
        q_dec = jnp.exp(lg * (p + 1.0))
        k_dec = jnp.exp(lg * (c - 1.0 - p))
        c_dec = jnp.exp(lg * float(c))
        return intra, q_dec, k_dec, c_dec

    dec_f = decays(log_g[0:1, 0:1], False)
    dec_b = decays(log_g[1:2, 0:1], True)

    def rows(chunk):
        return pl.ds(chunk * c, c)

    def scan_chunk(chunk, s, dec):
        intra, q_dec, k_dec, c_dec = dec
        qc = q_ref[rows(chunk), :]
        kc = k_ref[rows(chunk), :]
        vc = v_ref[rows(chunk), :]
        scores = (_dot_nt(qc, kc) * intra).astype(BF16)
        k_scaled = (kc.astype(F32) * k_dec).astype(BF16)
        if s is None:
            return _dot(scores, vc), _dot_tn(k_scaled, vc)
        q_scaled = (qc.astype(F32) * q_dec).astype(BF16)
        return _dot(scores, vc) + _dot(q_scaled, s.astype(BF16)), s * c_dec + _dot_tn(k_scaled, vc)

    def finalize(chunk, tot):
        nrm = tot * lax.rsqrt(jnp.mean(tot * tot, axis=-1, keepdims=True) + EPS)
        o_ref[0, 0, rows(chunk), :] = (g_ref[rows(chunk), :].astype(F32) * nrm).astype(BF16)

    fwd_order = list(range(N_CHUNKS))
    bwd_order = [0] + list(range(N_CHUNKS - 1, 0, -1))
    s_f = s_b = None
    fwd_seen, bwd_seen = set(), set()
    for cf, cb in zip(fwd_order, bwd_order):
        o_f, s_f = scan_chunk(cf, s_f, dec_f)
        o_b, s_b = scan_chunk(cb, s_b, dec_b)
        fwd_seen.add(cf)
        bwd_seen.add(cb)
        if cf == cb:
            finalize(cf, o_f + o_b)
            continue
        if cf in bwd_seen:
            finalize(cf, o_f + ob_scr[rows(cf), :])
        else:
            of_scr[rows(cf), :] = o_f
        if cb in fwd_seen:
            finalize(cb, of_scr[rows(cb), :] + o_b)
        else:
            ob_scr[rows(cb), :] = o_b


def _ret_core(qk, v, g, raw):
    nh = RET_HEADS
    return pl.pallas_call(
        _ret_kernel,
        grid=(BATCH, RET_HEADS),
        in_specs=[
            pl.BlockSpec((None, None, TOK, RET_QK_DIM), lambda b, h: (b, h, 0, 0)),
            pl.BlockSpec((None, None, TOK, RET_QK_DIM), lambda b, h: (b, nh + h, 0, 0)),
            pl.BlockSpec((None, None, TOK, RET_V_DIM), lambda b, h: (b, h, 0, 0)),
            pl.BlockSpec((None, None, TOK, RET_V_DIM), lambda b, h: (b, h, 0, 0)),
            pl.BlockSpec((1, 2, LANES), lambda b, h: (h, 0, 0)),
        ],
        out_specs=pl.BlockSpec((1, 1, TOK, RET_V_DIM), lambda b, h: (b, h, 0, 0)),
        out_shape=jax.ShapeDtypeStruct((BATCH, RET_HEADS, TOK, RET_V_DIM), BF16),
        scratch_shapes=[pltpu.VMEM((TOK, RET_V_DIM), F32), pltpu.VMEM((TOK, RET_V_DIM), F32)],
        compiler_params=pltpu.CompilerParams(
            dimension_semantics=("arbitrary", "arbitrary"), vmem_limit_bytes=VMEM_LIMIT),
        name="ret_core",
    )(qk, qk, v, g, raw)


def _na_key_start(i):
    return int(np.clip(NA_QROWS * i - NA_WIN_H // 2, 0, GRID_H - NA_KROWS))


def _na_row_geometry(i, a, j):
    r = NA_QROWS * i + a
    rs = int(np.clip(r - NA_WIN_H // 2, 0, GRID_H - NA_WIN_H))
    kr = _na_key_start(i) + j
    return rs <= kr < rs + NA_WIN_H, kr - r + NA_WIN_H - 1


NA_KIND_BLOCKS = (0, 1, NA_NBLK - 1)
for _i in range(2, NA_NBLK - 1):
    assert all(_na_row_geometry(_i, _a, _j) == _na_row_geometry(1, _a, _j)
               for _a in range(NA_QROWS) for _j in range(NA_KROWS))


def _rpb_pair_table(rpb):
    mid = NA_WIN_W - 1
    pad = jnp.zeros(rpb.shape[:2] + (LANES - rpb.shape[2],), rpb.dtype)
    centred = jnp.concatenate([rpb[..., mid:], pad, rpb[..., :mid]], axis=-1)
    shifted = jnp.roll(centred, GRID_W, axis=-1)
    lane = jnp.arange(LANES)
    around_64 = (lane >= GRID_W // 2) & (lane < LANES - GRID_W // 2)
    pair = jnp.where(around_64, shifted[:, 1:], centred[:, :-1])
    return jnp.concatenate([pair, jnp.zeros((rpb.shape[0], 2, LANES), rpb.dtype)], axis=1)


def _na_build_bias(pair_ref, bias_scr):
    c = lax.broadcasted_iota(jnp.int32, (GRID_W, LANES), 0)
    lane = lax.broadcasted_iota(jnp.int32, (GRID_W, LANES), 1)
    kc = lane & (GRID_W - 1)
    ws = jnp.clip(c - NA_WIN_W // 2, 0, GRID_W - NA_WIN_W)
    in_cols = (kc >= ws) & (kc < ws + NA_WIN_W)
    first = lane < GRID_W
    masks = {(True, True): in_cols,
             (True, False): in_cols & first,
             (False, True): in_cols & jnp.logical_not(first)}
    toeplitz = {}
    for kind, i in enumerate(NA_KIND_BLOCKS):
        for a in range(NA_QROWS):
            for grp in range(NA_KROWS // 2):
                v0, d0 = _na_row_geometry(i, a, 2 * grp)
                v1, _ = _na_row_geometry(i, a, 2 * grp + 1)
                if v0 or v1:
                    assert 0 <= d0 <= 2 * NA_WIN_H - 3
                    if d0 not in toeplitz:
                        base = jnp.broadcast_to(pair_ref[d0:d0 + 1, :], (GRID_W, LANES))
                        toeplitz[d0] = pltpu.roll(base, 0, 1, stride=1, stride_axis=0) * LOG2E
                    tile = jnp.where(masks[(v0, v1)], toeplitz[d0], NEG_BIG)
                else:
                    tile = jnp.full((GRID_W, LANES), NEG_BIG, F32)
                bias_scr[kind, a * GRID_W:(a + 1) * GRID_W, grp * LANES:(grp + 1) * LANES] = tile


def _na_kernel(q_ref, k_ref, v_ref, g_ref, pair_ref, o_ref, bias_ref, v1_scr):
    d = NA_HEAD_DIM

    @pl.when(pl.program_id(1) == 0)
    def _():
        for hd in range(NA_HPS):
            _na_build_bias(pair_ref.at[hd], bias_ref.at[hd])
            v1_scr[hd, :, d:] = jnp.ones((TOK, d), BF16)

    for hd in range(NA_HPS):
        v1_scr[hd, :, :d] = v_ref[hd]

    def finish(hd, o1, r0, rows):
        gate = g_ref[hd, r0:r0 + rows, :].astype(F32)
        o_ref[0, hd, r0:r0 + rows, :] = (gate * (o1[:, :d] / o1[:, d:])).astype(BF16)

    def scores(job):
        hd, i = job
        kc = k_ref[hd, 0:CTX_LEN, :]
        if i is None:
            return None, _dot_nt(q_ref[hd, 0:CTX_LEN, :], kc)
        kind = 0 if i == 0 else (2 if i == NA_NBLK - 1 else 1)
        r0 = CTX_LEN + i * NA_QBLK
        k0 = CTX_LEN + _na_key_start(i) * GRID_W
        q = q_ref[hd, r0:r0 + NA_QBLK, :]
        return _dot_nt(q, k_ref[hd, k0:k0 + NA_KBLK, :]) + bias_ref[hd, kind], _dot_nt(q, kc)

    def attend(job, s_loc, s_ctx):
        hd, i = job
        vc = v1_scr[hd, 0:CTX_LEN, :]
        m = jnp.max(s_ctx, axis=-1, keepdims=True)
        if i is None:
            finish(hd, _dot(jnp.exp2(s_ctx - m).astype(BF16), vc), 0, CTX_LEN)
            return
        r0 = CTX_LEN + i * NA_QBLK
        k0 = CTX_LEN + _na_key_start(i) * GRID_W
        m = jnp.maximum(jnp.max(s_loc, axis=-1, keepdims=True), m)
        p_loc = jnp.exp2(s_loc - m).astype(BF16)
        p_ctx = jnp.exp2(s_ctx - m).astype(BF16)
        finish(hd, _dot(p_loc, v1_scr[hd, k0:k0 + NA_KBLK, :]) + _dot(p_ctx, vc), r0, NA_QBLK)

    blocks = list(range(NA_NBLK))
    blocks.insert(NA_NBLK // 2, None)
    order = [(hd, i) for i in blocks for hd in range(NA_HPS)]
    pending = [scores(job) for job in order[:NA_LOOKAHEAD]]
    for n, job in enumerate(order):
        s_loc, s_ctx = pending.pop(0)
        if n + NA_LOOKAHEAD < len(order):
            pending.append(scores(order[n + NA_LOOKAHEAD]))
        attend(job, s_loc, s_ctx)


def _na_core(qkv, g, rpb_pairs):
    nblk = NA_HEADS // NA_HPS
    heads = (None, NA_HPS, TOK, NA_HEAD_DIM)
    return pl.pallas_call(
        _na_kernel,
        grid=(nblk, BATCH),
        in_specs=[
            pl.BlockSpec(heads, lambda h, b: (b, h, 0, 0)),
            pl.BlockSpec(heads, lambda h, b: (b, nblk + h, 0, 0)),
            pl.BlockSpec(heads, lambda h, b: (b, 2 * nblk + h, 0, 0)),
            pl.BlockSpec(heads, lambda h, b: (b, h, 0, 0)),
            pl.BlockSpec((NA_HPS, 2 * NA_WIN_H, LANES), lambda h, b: (h, 0, 0)),
        ],
        out_specs=pl.BlockSpec((1, NA_HPS, TOK, NA_HEAD_DIM), lambda h, b: (b, h, 0, 0)),
        out_shape=jax.ShapeDtypeStruct((BATCH, NA_HEADS, TOK, NA_HEAD_DIM), BF16),
        scratch_shapes=[pltpu.VMEM((NA_HPS, len(NA_KIND_BLOCKS), NA_QBLK, NA_KBLK), F32),
                        pltpu.VMEM((NA_HPS, TOK, 2 * NA_HEAD_DIM), BF16)],
        compiler_params=pltpu.CompilerParams(
            dimension_semantics=("arbitrary", "arbitrary"), vmem_limit_bytes=VMEM_LIMIT),
        name="na_core",
    )(qkv, qkv, qkv, g, rpb_pairs)


def _final_norm_kernel(x_ref, g_ref, o_ref):
    x = x_ref[0]
    o_ref[0] = x * lax.rsqrt(jnp.mean(x * x, axis=-1, keepdims=True) + EPS) * g_ref[...]


def _final_norm(xs, g):
    tm = CTX_LEN
    return pl.pallas_call(
        _final_norm_kernel,
        grid=(BATCH, SEQ // tm),
        in_specs=[
            pl.BlockSpec((1, tm, D_MODEL), lambda b, i: (b, i + CTX_LEN // tm, 0)),
            pl.BlockSpec((1, D_MODEL), lambda b, i: (0, 0)),
        ],
        out_specs=pl.BlockSpec((1, tm, D_MODEL), lambda b, i: (b, i, 0)),
        out_shape=jax.ShapeDtypeStruct((BATCH, SEQ, D_MODEL), F32),
        compiler_params=pltpu.CompilerParams(
            dimension_semantics=("arbitrary", "arbitrary"), vmem_limit_bytes=VMEM_LIMIT),
        name="final_norm",
    )(xs, g)


def _rope_tables():
    quarter = RET_QK_DIM // 4
    freqs = ROPE_BASE ** (-jnp.arange(quarter, dtype=F32) / quarter)
    t = jnp.arange(SEQ)
    ang_r = (t // GRID_W).astype(F32)[:, None] * freqs[None, :]
    ang_c = (t % GRID_W).astype(F32)[:, None] * freqs[None, :]
    cos = jnp.concatenate([jnp.cos(ang_r)] * 2 + [jnp.cos(ang_c)] * 2, axis=-1)
    sin = jnp.concatenate([-jnp.sin(ang_r), jnp.sin(ang_r), -jnp.sin(ang_c), jnp.sin(ang_c)], axis=-1)
    cos = jnp.concatenate([jnp.ones((CTX_LEN, RET_QK_DIM), F32), cos], axis=0)
    sin = jnp.concatenate([jnp.zeros((CTX_LEN, RET_QK_DIM), F32), sin], axis=0)
    return cos, sin


def _per_batch(m):
    ctx_row = jnp.broadcast_to(m[BATCH][None, None, :], (BATCH, 1, m.shape[-1]))
    return jnp.concatenate([ctx_row, m[:BATCH][:, None, :]], axis=1)


def kernel(x, c, ctx, c_ctx, mod_w, mod_b, norm_g, ret_w_in, ret_decay_fwd, ret_decay_bwd, ret_w_out,
           na_w_in, na_rpb, na_w_out, final_g):
    xs = jnp.concatenate([ctx, x], axis=1)
    cvec = jnp.concatenate([c, c_ctx[None], jnp.zeros((8 - BATCH - 1, D_MODEL), F32)], axis=0)
    mods = _modulation(cvec, mod_w, mod_b)
    cos_t, sin_t = _rope_tables()

    for l in range(DEPTH):
        j = l // 2
        sh = _per_batch(mods[l, :, :D_MODEL])
        sc = _per_batch(mods[l, :, D_MODEL:2 * D_MODEL])
        gt = _per_batch(mods[l, :, 2 * D_MODEL:])
        h = _norm_mod(xs, norm_g[l][None, :], sc, sh)
        if l % 2 == 0:
            qk = _in_proj(_in_proj_rope_kernel, "in_proj_ret_qk", h, ret_w_in, j, 0, 2 * D_MODEL, RET_QK_DIM,
                          (cos_t, sin_t))
            plain = functools.partial(_in_proj_scaled_kernel, n_scaled=0, scale=1.0)
            v = _in_proj(plain, "in_proj_ret_v", h, ret_w_in, j, 2 * D_MODEL, D_BRANCH, RET_V_DIM)
            g = _in_proj(_in_proj_silu_kernel, "in_proj_ret_g", h, ret_w_in, j, 2 * D_MODEL + D_BRANCH, D_BRANCH,
                         RET_V_DIM)
            raw = jnp.stack([ret_decay_fwd[j], ret_decay_bwd[j]], axis=1)
            raw = jnp.broadcast_to(raw[:, :, None], (RET_HEADS, 2, LANES))
            a = _ret_core(qk, v, g, raw)
            xs = _out_proj(a, ret_w_out, j, xs, gt)
        else:
            scaled_q = functools.partial(_in_proj_scaled_kernel, n_scaled=D_BRANCH // TN_IN,
                                         scale=NA_HEAD_DIM ** -0.5 * LOG2E)
            qkv = _in_proj(scaled_q, "in_proj_na_qkv", h, na_w_in, j, 0, 3 * D_BRANCH, NA_HEAD_DIM)
            g = _in_proj(_in_proj_silu_kernel, "in_proj_na_g", h, na_w_in, j, 3 * D_BRANCH, D_BRANCH, NA_HEAD_DIM)
            a = _na_core(qkv, g, _rpb_pair_table(na_rpb[j]))
            xs = _out_proj(a, na_w_out, j, xs, gt)
    return _final_norm(xs, final_g[None, :])
```

```python
import functools

import numpy as np
import jax
import jax.numpy as jnp
from jax import lax
from jax.experimental import pallas as pl
from jax.experimental.pallas import tpu as pltpu

D_MODEL = 2048
BATCH = 4
SEQ = 2048
DEPTH = 4
GRID_W = 64
GRID_H = SEQ // GRID_W
CTX_LEN = 256
TOK = CTX_LEN + SEQ
D_BRANCH = 2 * D_MODEL
RET_HEADS = 8
RET_QK_DIM = D_MODEL // RET_HEADS
RET_V_DIM = D_BRANCH // RET_HEADS
NA_HEADS = 32
NA_HEAD_DIM = D_BRANCH // NA_HEADS
NA_WIN_H = 8
NA_WIN_W = 16
ROPE_BASE = 10000.0
EPS = 1e-6

F32 = jnp.float32
BF16 = jnp.bfloat16

LANES = 128
VMEM_LIMIT = 58 * 1024 * 1024

TM = TOK
TN_IN = 1024
MM_ROWS = TOK // 8
TM_OUT = TOK // 2
TN_OUT = 512
TM_NORM = TOK // 3
NORM_SLAB = 32
RET_CHUNK = 256
N_CHUNKS = TOK // RET_CHUNK
RET_PARKED = (N_CHUNKS - 1) // 2
RET_HPS = 2

NA_QROWS = 4
NA_KROWS = 12
NA_QBLK = NA_QROWS * GRID_W
NA_KBLK = NA_KROWS * GRID_W
NA_NBLK = GRID_H // NA_QROWS
NA_HPS = 4
NA_LOOKAHEAD = 4 * NA_HPS
NEG_BIG = -1e30
LOG2E = 1.4426950408889634


def _dot(a, b):
    return jnp.dot(a, b, preferred_element_type=F32)


def _dot_nt(a, b):
    return lax.dot_general(a, b, (((1,), (1,)), ((), ())), preferred_element_type=F32)


def _dot_tn(a, b):
    return lax.dot_general(a, b, (((0,), (0,)), ((), ())), preferred_element_type=F32)


def _silu(x):
    half = 0.5 * x
    return half + half * jnp.tanh(half)


def _mod_kernel(c_ref, w_ref, b_ref, o_ref):
    cond = _silu(c_ref[...]).astype(BF16)
    o_ref[0] = _dot(cond, w_ref[0].astype(BF16)) + b_ref[0]


def _modulation(cvec, mod_w, mod_b):
    tn = 1024
    n = 3 * D_MODEL
    return pl.pallas_call(
        _mod_kernel,
        grid=(DEPTH, n // tn),
        in_specs=[
            pl.BlockSpec((8, D_MODEL), lambda l, j: (0, 0)),
            pl.BlockSpec((1, D_MODEL, tn), lambda l, j: (l, 0, j)),
            pl.BlockSpec((1, 1, tn), lambda l, j: (l, 0, j)),
        ],
        out_specs=pl.BlockSpec((1, 8, tn), lambda l, j: (l, 0, j)),
        out_shape=jax.ShapeDtypeStruct((DEPTH, 8, n), F32),
        compiler_params=pltpu.CompilerParams(
            dimension_semantics=("arbitrary", "arbitrary"), vmem_limit_bytes=VMEM_LIMIT),
        name="modulation",
    )(cvec, mod_w, mod_b.reshape(DEPTH, 1, n))


def _norm_mod_kernel(x_ref, g_ref, sc_ref, sh_ref, o_ref):
    ctx_row = jnp.where(pl.program_id(1) == 0, 0, 1)
    g = g_ref[...]
    gain = [g * (1.0 + sc_ref[0, pl.ds(row, 1), :]) for row in (ctx_row, 1)]
    shift = [sh_ref[0, pl.ds(row, 1), :] for row in (ctx_row, 1)]
    for r0 in range(0, TM_NORM, NORM_SLAB):
        which = 0 if r0 < CTX_LEN else 1
        x = x_ref[0, r0:r0 + NORM_SLAB, :]
        y = x * lax.rsqrt(jnp.mean(x * x, axis=-1, keepdims=True) + EPS)
        o_ref[0, r0:r0 + NORM_SLAB, :] = (y * gain[which] + shift[which]).astype(BF16)


def _norm_mod(xs, g, sc, sh):
    tm = TM_NORM
    return pl.pallas_call(
        _norm_mod_kernel,
        grid=(BATCH, TOK // tm),
        in_specs=[
            pl.BlockSpec((1, tm, D_MODEL), lambda b, m: (b, m, 0)),
            pl.BlockSpec((1, D_MODEL), lambda b, m: (0, 0)),
            pl.BlockSpec((1, 2, D_MODEL), lambda b, m: (b, 0, 0)),
            pl.BlockSpec((1, 2, D_MODEL), lambda b, m: (b, 0, 0)),
        ],
        out_specs=pl.BlockSpec((1, tm, D_MODEL), lambda b, m: (b, m, 0)),
        out_shape=jax.ShapeDtypeStruct((BATCH, TOK, D_MODEL), BF16),
        compiler_params=pltpu.CompilerParams(
            dimension_semantics=("arbitrary", "arbitrary"), vmem_limit_bytes=VMEM_LIMIT),
        name="norm_mod",
    )(xs, g, sc, sh)


def _first_token_tile():
    return (pl.program_id(1) == 0) & (pl.program_id(2) == 0)


def _resident_matmul(h_ref, w_ref, wb_scr):
    @pl.when(_first_token_tile())
    def _():
        wb_scr[...] = w_ref[0].astype(BF16)

    return [(r0, _dot(h_ref[0, r0:r0 + MM_ROWS, :], wb_scr[...])) for r0 in range(0, h_ref.shape[1], MM_ROWS)]


def _store_heads(o_ref, r0, c0, val):
    hw = o_ref.shape[-1]
    rows, w = val.shape
    if w <= hw:
        o_ref[0, c0 // hw, r0:r0 + rows, c0 % hw:c0 % hw + w] = val
    else:
        for k in range(w // hw):
            o_ref[0, c0 // hw + k, r0:r0 + rows, :] = val[:, k * hw:(k + 1) * hw]


def _in_proj_scaled_kernel(h_ref, w_ref, o_ref, wb_scr, *, n_scaled, scale):
    factor = jnp.where(pl.program_id(0) < n_scaled, scale, 1.0)
    for r0, acc in _resident_matmul(h_ref, w_ref, wb_scr):
        _store_heads(o_ref, r0, 0, (acc * factor).astype(BF16))


def _in_proj_silu_kernel(h_ref, w_ref, o_ref, wb_scr):
    for r0, acc in _resident_matmul(h_ref, w_ref, wb_scr):
        _store_heads(o_ref, r0, 0, _silu(acc).astype(BF16))


def _in_proj_rope_kernel(h_ref, w_ref, cos_ref, sin_ref, o_ref, wb_scr):
    scale = jnp.where(pl.program_id(0) < D_MODEL // TN_IN, 1.0, RET_QK_DIM ** -0.5)
    for r0, acc in _resident_matmul(h_ref, w_ref, wb_scr):
        rows = slice(r0, r0 + MM_ROWS)
        for c0 in range(0, TN_IN, LANES):
            t0 = c0 % RET_QK_DIM
            xg = acc[:, c0:c0 + LANES]
            r = xg * cos_ref[rows, t0:t0 + LANES] + pltpu.roll(xg, LANES // 2, 1) * sin_ref[rows, t0:t0 + LANES]
            _store_heads(o_ref, r0, c0, (r * scale).astype(BF16))


def _in_proj(body, name, h, w_all, layer, col0, ncols, head_w, tables=()):
    tile0 = col0 // TN_IN
    tm = TM
    table_spec = pl.BlockSpec((tm, RET_QK_DIM), lambda j, b, m: (m, 0), pipeline_mode=pl.Buffered(1))
    return pl.pallas_call(
        body,
        grid=(ncols // TN_IN, BATCH, TOK // tm),
        in_specs=[
            pl.BlockSpec((1, tm, D_MODEL), lambda j, b, m: (b, m, 0)),
            pl.BlockSpec((1, D_MODEL, TN_IN), lambda j, b, m: (layer, 0, tile0 + j)),
        ] + [table_spec] * len(tables),
        out_specs=pl.BlockSpec((1, TN_IN // head_w, tm, head_w), lambda j, b, m: (b, j, m, 0)),
        out_shape=jax.ShapeDtypeStruct((BATCH, ncols // head_w, TOK, head_w), BF16),
        scratch_shapes=[pltpu.VMEM((D_MODEL, TN_IN), BF16)],
        compiler_params=pltpu.CompilerParams(
            dimension_semantics=("arbitrary", "arbitrary", "arbitrary"), vmem_limit_bytes=VMEM_LIMIT),
        name=name,
    )(h, w_all, *tables)


def _out_proj_kernel(a_ref, w_ref, x_ref, gt_ref, o_ref, wb_scr):
    @pl.when(_first_token_tile())
    def _():
        wb_scr[...] = w_ref[0].astype(BF16)

    for r0 in range(0, TM_OUT, MM_ROWS):
        rows = slice(r0, r0 + MM_ROWS)
        a = jnp.concatenate([a_ref[0, hd, rows, :] for hd in range(a_ref.shape[1])], axis=-1)
        y = _dot(a, wb_scr[...])
        tok = pl.program_id(2) * TM_OUT + r0 + lax.broadcasted_iota(jnp.int32, (MM_ROWS, 1), 0)
        gt = jnp.where(tok < CTX_LEN, gt_ref[0, 0:1, :], gt_ref[0, 1:2, :])
        o_ref[0, rows, :] = x_ref[0, rows, :] + gt * y


def _out_proj(a, w_all, layer, xs, gt):
    n_heads, head_w = a.shape[1], a.shape[3]
    return pl.pallas_call(
        _out_proj_kernel,
        grid=(D_MODEL // TN_OUT, BATCH, TOK // TM_OUT),
        in_specs=[
            pl.BlockSpec((1, n_heads, TM_OUT, head_w), lambda j, b, m: (b, 0, m, 0)),
            pl.BlockSpec((1, D_BRANCH, TN_OUT), lambda j, b, m: (layer, 0, j)),
            pl.BlockSpec((1, TM_OUT, TN_OUT), lambda j, b, m: (b, m, j)),
            pl.BlockSpec((1, 2, TN_OUT), lambda j, b, m: (b, 0, j)),
        ],
        out_specs=pl.BlockSpec((1, TM_OUT, TN_OUT), lambda j, b, m: (b, m, j)),
        out_shape=jax.ShapeDtypeStruct((BATCH, TOK, D_MODEL), F32),
        scratch_shapes=[pltpu.VMEM((D_BRANCH, TN_OUT), BF16)],
        input_output_aliases={2: 0},
        compiler_params=pltpu.CompilerParams(
            dimension_semantics=("arbitrary", "arbitrary", "arbitrary"), vmem_limit_bytes=VMEM_LIMIT),
        name="out_proj",
    )(a, w_all, xs, gt)


def _ret_kernel(q_ref, k_ref, v_ref, g_ref, raw_ref, o_ref, of_scr, ob_scr):
    c = RET_CHUNK
    pos_i = lax.broadcasted_iota(jnp.int32, (c, 1), 0).astype(F32)
    ii = lax.broadcasted_iota(jnp.int32, (c, c), 0)
    jj = lax.broadcasted_iota(jnp.int32, (c, c), 1)
    diff = (ii - jj).astype(F32)

    def decays(lg, reverse):
        d = -diff if reverse else diff
        intra = jnp.where(d >= 0, jnp.exp(lg * jnp.maximum(d, 0.0)), 0.0)
        p = (c - 1.0 - pos_i) if reverse else pos_i
        q_dec = jnp.exp(lg * (p + 1.0))
        k_dec = jnp.exp(lg * (c - 1.0 - p))
        c_dec = jnp.exp(lg * float(c))
        return intra, q_dec, k_dec, c_dec

    def head_decays(hd):
        raw = raw_ref[hd]
        log_g = -(jnp.maximum(raw, 0.0) + jnp.log1p(jnp.exp(-jnp.abs(raw))))
        return decays(log_g[0:1, 0:1], False), decays(log_g[1:2, 0:1], True)

    def rows(chunk):
        return pl.ds(chunk * c, c)

    def scan_chunk(hd, chunk, s, dec):
        intra, q_dec, k_dec, c_dec = dec
        qc = q_ref[hd, rows(chunk), :]
        kc = k_ref[hd, rows(chunk), :]
        vc = v_ref[hd, rows(chunk), :]
        scores = (_dot_nt(qc, kc) * intra).astype(BF16)
        k_scaled = (kc.astype(F32) * k_dec).astype(BF16)
        if s is None:
            return _dot(scores, vc), _dot_tn(k_scaled, vc)
        q_scaled = (qc.astype(F32) * q_dec).astype(BF16)
        return _dot(scores, vc) + _dot(q_scaled, s.astype(BF16)), s * c_dec + _dot_tn(k_scaled, vc)

    def finalize(hd, chunk, tot):
        nrm = tot * lax.rsqrt(jnp.mean(tot * tot, axis=-1, keepdims=True) + EPS)
        o_ref[0, hd, rows(chunk), :] = (g_ref[hd, rows(chunk), :].astype(F32) * nrm).astype(BF16)

    fwd_order = list(range(N_CHUNKS))
    bwd_order = [0] + list(range(N_CHUNKS - 1, 0, -1))
    fwd_slot = {ch: n for n, ch in enumerate(fwd_order[1:1 + RET_PARKED])}
    bwd_slot = {ch: n for n, ch in enumerate(bwd_order[1:1 + RET_PARKED])}
    dec = [head_decays(hd) for hd in range(RET_HPS)]
    s_f = [None] * RET_HPS
    s_b = [None] * RET_HPS
    for cf, cb in zip(fwd_order, bwd_order):
        for hd in range(RET_HPS):
            o_f, s_f[hd] = scan_chunk(hd, cf, s_f[hd], dec[hd][0])
            o_b, s_b[hd] = scan_chunk(hd, cb, s_b[hd], dec[hd][1])
            if cf == cb:
                finalize(hd, cf, o_f + o_b)
                continue
            if cf in fwd_slot:
                of_scr[hd, rows(fwd_slot[cf]), :] = o_f
            else:
                finalize(hd, cf, o_f + ob_scr[hd, rows(bwd_slot[cf]), :])
            if cb in bwd_slot:
                ob_scr[hd, rows(bwd_slot[cb]), :] = o_b
            else:
                finalize(hd, cb, of_scr[hd, rows(fwd_slot[cb]), :] + o_b)


def _ret_core(qk, v, g, raw):
    nblk = RET_HEADS // RET_HPS
    return pl.pallas_call(
        _ret_kernel,
        grid=(BATCH, nblk),
        in_specs=[
            pl.BlockSpec((None, RET_HPS, TOK, RET_QK_DIM), lambda b, h: (b, h, 0, 0)),
            pl.BlockSpec((None, RET_HPS, TOK, RET_QK_DIM), lambda b, h: (b, nblk + h, 0, 0)),
            pl.BlockSpec((None, RET_HPS, TOK, RET_V_DIM), lambda b, h: (b, h, 0, 0)),
            pl.BlockSpec((None, RET_HPS, TOK, RET_V_DIM), lambda b, h: (b, h, 0, 0)),
            pl.BlockSpec((RET_HPS, 2, LANES), lambda b, h: (h, 0, 0)),
        ],
        out_specs=pl.BlockSpec((1, RET_HPS, TOK, RET_V_DIM), lambda b, h: (b, h, 0, 0)),
        out_shape=jax.ShapeDtypeStruct((BATCH, RET_HEADS, TOK, RET_V_DIM), BF16),
        scratch_shapes=[pltpu.VMEM((RET_HPS, RET_PARKED * RET_CHUNK, RET_V_DIM), F32),
                        pltpu.VMEM((RET_HPS, RET_PARKED * RET_CHUNK, RET_V_DIM), F32)],
        compiler_params=pltpu.CompilerParams(
            dimension_semantics=("arbitrary", "arbitrary"), vmem_limit_bytes=VMEM_LIMIT),
        name="ret_core",
    )(qk, qk, v, g, raw)


def _na_key_start(i):
    return int(np.clip(NA_QROWS * i - NA_WIN_H // 2, 0, GRID_H - NA_KROWS))


def _na_row_geometry(i, a, j):
    r = NA_QROWS * i + a
    rs = int(np.clip(r - NA_WIN_H // 2, 0, GRID_H - NA_WIN_H))
    kr = _na_key_start(i) + j
    return rs <= kr < rs + NA_WIN_H, kr - r + NA_WIN_H - 1


NA_KIND_BLOCKS = (0, 1, NA_NBLK - 1)
for _i in range(2, NA_NBLK - 1):
    assert all(_na_row_geometry(_i, _a, _j) == _na_row_geometry(1, _a, _j)
               for _a in range(NA_QROWS) for _j in range(NA_KROWS))


def _rpb_pair_table(rpb):
    mid = NA_WIN_W - 1
    pad = jnp.zeros(rpb.shape[:2] + (LANES - rpb.shape[2],), rpb.dtype)
    centred = jnp.concatenate([rpb[..., mid:], pad, rpb[..., :mid]], axis=-1)
    shifted = jnp.roll(centred, GRID_W, axis=-1)
    lane = jnp.arange(LANES)
    around_64 = (lane >= GRID_W // 2) & (lane < LANES - GRID_W // 2)
    pair = jnp.where(around_64, shifted[:, 1:], centred[:, :-1])
    return jnp.concatenate([pair, jnp.zeros((rpb.shape[0], 2, LANES), rpb.dtype)], axis=1)


def _na_build_bias(pair_ref, bias_scr):
    c = lax.broadcasted_iota(jnp.int32, (GRID_W, LANES), 0)
    lane = lax.broadcasted_iota(jnp.int32, (GRID_W, LANES), 1)
    kc = lane & (GRID_W - 1)
    ws = jnp.clip(c - NA_WIN_W // 2, 0, GRID_W - NA_WIN_W)
    in_cols = (kc >= ws) & (kc < ws + NA_WIN_W)
    first = lane < GRID_W
    masks = {(True, True): in_cols,
             (True, False): in_cols & first,
             (False, True): in_cols & jnp.logical_not(first)}
    toeplitz = {}
    for kind, i in enumerate(NA_KIND_BLOCKS):
        for a in range(NA_QROWS):
            for grp in range(NA_KROWS // 2):
                v0, d0 = _na_row_geometry(i, a, 2 * grp)
                v1, _ = _na_row_geometry(i, a, 2 * grp + 1)
                if v0 or v1:
                    assert 0 <= d0 <= 2 * NA_WIN_H - 3
                    if d0 not in toeplitz:
                        base = jnp.broadcast_to(pair_ref[d0:d0 + 1, :], (GRID_W, LANES))
                        toeplitz[d0] = pltpu.roll(base, 0, 1, stride=1, stride_axis=0) * LOG2E
                    tile = jnp.where(masks[(v0, v1)], toeplitz[d0], NEG_BIG)
                else:
                    tile = jnp.full((GRID_W, LANES), NEG_BIG, F32)
                bias_scr[kind, a * GRID_W:(a + 1) * GRID_W, grp * LANES:(grp + 1) * LANES] = tile


def _na_kernel(q_ref, k_ref, v_ref, g_ref, pair_ref, o_ref, bias_ref, v1_scr):
    d = NA_HEAD_DIM

    @pl.when(pl.program_id(1) == 0)
    def _():
        for hd in range(NA_HPS):
            _na_build_bias(pair_ref.at[hd], bias_ref.at[hd])
            v1_scr[hd, :, d:] = jnp.ones((TOK, d), BF16)

    for hd in range(NA_HPS):
        v1_scr[hd, :, :d] = v_ref[hd]

    def finish(hd, o1, r0, rows):
        gate = g_ref[hd, r0:r0 + rows, :].astype(F32)
        o_ref[0, hd, r0:r0 + rows, :] = (gate * (o1[:, :d] / o1[:, d:])).astype(BF16)

    def scores(job):
        hd, i = job
        kc = k_ref[hd, 0:CTX_LEN, :]
        if i is None:
            return None, _dot_nt(q_ref[hd, 0:CTX_LEN, :], kc)
        kind = 0 if i == 0 else (2 if i == NA_NBLK - 1 else 1)
        r0 = CTX_LEN + i * NA_QBLK
        k0 = CTX_LEN + _na_key_start(i) * GRID_W
        q = q_ref[hd, r0:r0 + NA_QBLK, :]
        return _dot_nt(q, k_ref[hd, k0:k0 + NA_KBLK, :]) + bias_ref[hd, kind], _dot_nt(q, kc)

    def attend(job, s_loc, s_ctx):
        hd, i = job
        vc = v1_scr[hd, 0:CTX_LEN, :]
        m = jnp.max(s_ctx, axis=-1, keepdims=True)
        if i is None:
            finish(hd, _dot(jnp.exp2(s_ctx - m).astype(BF16), vc), 0, CTX_LEN)
            return
        r0 = CTX_LEN + i * NA_QBLK
        k0 = CTX_LEN + _na_key_start(i) * GRID_W
        m = jnp.maximum(jnp.max(s_loc, axis=-1, keepdims=True), m)
        p_loc = jnp.exp2(s_loc - m).astype(BF16)
        p_ctx = jnp.exp2(s_ctx - m).astype(BF16)
        finish(hd, _dot(p_loc, v1_scr[hd, k0:k0 + NA_KBLK, :]) + _dot(p_ctx, vc), r0, NA_QBLK)

    blocks = list(range(NA_NBLK))
    blocks.insert(NA_NBLK // 2, None)
    order = [(hd, i) for i in blocks for hd in range(NA_HPS)]
    pending = [scores(job) for job in order[:NA_LOOKAHEAD]]
    for n, job in enumerate(order):
        s_loc, s_ctx = pending.pop(0)
        if n + NA_LOOKAHEAD < len(order):
            pending.append(scores(order[n + NA_LOOKAHEAD]))
        attend(job, s_loc, s_ctx)


def _na_core(qkv, g, rpb_pairs):
    nblk = NA_HEADS // NA_HPS
    heads = (None, NA_HPS, TOK, NA_HEAD_DIM)
    return pl.pallas_call(
        _na_kernel,
        grid=(nblk, BATCH),
        in_specs=[
            pl.BlockSpec(heads, lambda h, b: (b, h, 0, 0)),
            pl.BlockSpec(heads, lambda h, b: (b, nblk + h, 0, 0)),
            pl.BlockSpec(heads, lambda h, b: (b, 2 * nblk + h, 0, 0)),
            pl.BlockSpec(heads, lambda h, b: (b, h, 0, 0)),
            pl.BlockSpec((NA_HPS, 2 * NA_WIN_H, LANES), lambda h, b: (h, 0, 0)),
        ],
        out_specs=pl.BlockSpec((1, NA_HPS, TOK, NA_HEAD_DIM), lambda h, b: (b, h, 0, 0)),
        out_shape=jax.ShapeDtypeStruct((BATCH, NA_HEADS, TOK, NA_HEAD_DIM), BF16),
        scratch_shapes=[pltpu.VMEM((NA_HPS, len(NA_KIND_BLOCKS), NA_QBLK, NA_KBLK), F32),
                        pltpu.VMEM((NA_HPS, TOK, 2 * NA_HEAD_DIM), BF16)],
        compiler_params=pltpu.CompilerParams(
            dimension_semantics=("arbitrary", "arbitrary"), vmem_limit_bytes=VMEM_LIMIT),
        name="na_core",
    )(qkv, qkv, qkv, g, rpb_pairs)


def _final_norm_kernel(x_ref, g_ref, o_ref):
    x = x_ref[0]
    o_ref[0] = x * lax.rsqrt(jnp.mean(x * x, axis=-1, keepdims=True) + EPS) * g_ref[...]


def _final_norm(xs, g):
    tm = CTX_LEN
    return pl.pallas_call(
        _final_norm_kernel,
        grid=(BATCH, SEQ // tm),
        in_specs=[
            pl.BlockSpec((1, tm, D_MODEL), lambda b, i: (b, i + CTX_LEN // tm, 0)),
            pl.BlockSpec((1, D_MODEL), lambda b, i: (0, 0)),
        ],
        out_specs=pl.BlockSpec((1, tm, D_MODEL), lambda b, i: (b, i, 0)),
        out_shape=jax.ShapeDtypeStruct((BATCH, SEQ, D_MODEL), F32),
        compiler_params=pltpu.CompilerParams(
            dimension_semantics=("arbitrary", "arbitrary"), vmem_limit_bytes=VMEM_LIMIT),
        name="final_norm",
    )(xs, g)


def _rope_tables():
    quarter = RET_QK_DIM // 4
    f32 = np.float32
    freqs = f32(ROPE_BASE) ** (-np.arange(quarter, dtype=f32) / f32(quarter))
    t = np.arange(SEQ)
    ang_r = (t // GRID_W).astype(f32)[:, None] * freqs[None, :]
    ang_c = (t % GRID_W).astype(f32)[:, None] * freqs[None, :]
    cos = np.concatenate([np.cos(ang_r)] * 2 + [np.cos(ang_c)] * 2, axis=-1)
    sin = np.concatenate([-np.sin(ang_r), np.sin(ang_r), -np.sin(ang_c), np.sin(ang_c)], axis=-1)
    cos = np.concatenate([np.ones((CTX_LEN, RET_QK_DIM), f32), cos], axis=0)
    sin = np.concatenate([np.zeros((CTX_LEN, RET_QK_DIM), f32), sin], axis=0)
    return jnp.asarray(cos, F32), jnp.asarray(sin, F32)


def _per_batch(m):
    ctx_row = jnp.broadcast_to(m[BATCH][None, None, :], (BATCH, 1, m.shape[-1]))
    return jnp.concatenate([ctx_row, m[:BATCH][:, None, :]], axis=1)


def kernel(x, c, ctx, c_ctx, mod_w, mod_b, norm_g, ret_w_in, ret_decay_fwd, ret_decay_bwd, ret_w_out,
           na_w_in, na_rpb, na_w_out, final_g):
    xs = jnp.concatenate([ctx, x], axis=1)
    cvec = jnp.concatenate([c, c_ctx[None], jnp.zeros((8 - BATCH - 1, D_MODEL), F32)], axis=0)
    mods = _modulation(cvec, mod_w, mod_b)
    cos_t, sin_t = _rope_tables()

    for l in range(DEPTH):
        j = l // 2
        sh = _per_batch(mods[l, :, :D_MODEL])
        sc = _per_batch(mods[l, :, D_MODEL:2 * D_MODEL])
        gt = _per_batch(mods[l, :, 2 * D_MODEL:])
        h = _norm_mod(xs, norm_g[l][None, :], sc, sh)
        if l % 2 == 0:
            qk = _in_proj(_in_proj_rope_kernel, "in_proj_ret_qk", h, ret_w_in, j, 0, 2 * D_MODEL, RET_QK_DIM,
                          (cos_t, sin_t))
            plain = functools.partial(_in_proj_scaled_kernel, n_scaled=0, scale=1.0)
            v = _in_proj(plain, "in_proj_ret_v", h, ret_w_in, j, 2 * D_MODEL, D_BRANCH, RET_V_DIM)
            g = _in_proj(_in_proj_silu_kernel, "in_proj_ret_g", h, ret_w_in, j, 2 * D_MODEL + D_BRANCH, D_BRANCH,
                         RET_V_DIM)
            raw = jnp.stack([ret_decay_fwd[j], ret_decay_bwd[j]], axis=1)
            raw = jnp.broadcast_to(raw[:, :, None], (RET_HEADS, 2, LANES))
            a = _ret_core(qk, v, g, raw)
            xs = _out_proj(a, ret_w_out, j, xs, gt)
        else:
            scaled_q = functools.partial(_in_proj_scaled_kernel, n_scaled=D_BRANCH // TN_IN,
                                         scale=NA_HEAD_DIM ** -0.5 * LOG2E)
            qkv = _in_proj(scaled_q, "in_proj_na_qkv", h, na_w_in, j, 0, 3 * D_BRANCH, NA_HEAD_DIM)
            g = _in_proj(_in_proj_silu_kernel, "in_proj_na_g", h, na_w_in, j, 3 * D_BRANCH, D_BRANCH, NA_HEAD_DIM)
            a = _na_core(qkv, g, _rpb_pair_table(na_rpb[j]))
            xs = _out_proj(a, na_w_out, j, xs, gt)
    return _final_norm(xs, final_g[None, :])
```

```python
import functools

import numpy as np
import jax
import jax.numpy as jnp
from jax import lax
from jax.experimental import pallas as pl
from jax.experimental.pallas import tpu as pltpu

D_MODEL = 2048
BATCH = 4
SEQ = 2048
DEPTH = 4
GRID_W = 64
GRID_H = SEQ // GRID_W
CTX_LEN = 256
TOK = CTX_LEN + SEQ
D_BRANCH = 2 * D_MODEL
RET_HEADS = 8
RET_QK_DIM = D_MODEL // RET_HEADS
RET_V_DIM = D_BRANCH // RET_HEADS
NA_HEADS = 32
NA_HEAD_DIM = D_BRANCH // NA_HEADS
NA_WIN_H = 8
NA_WIN_W = 16
ROPE_BASE = 10000.0
EPS = 1e-6

F32 = jnp.float32
BF16 = jnp.bfloat16

LANES = 128
VMEM_LIMIT = 60 * 1024 * 1024

TM = TOK
TN_IN = 1024
MM_ROWS = TOK // 8
TM_OUT = TOK // 2
TN_OUT = 512
TM_NORM = TOK // 3
NORM_SLAB = 32
RET_CHUNK = 256
N_CHUNKS = TOK // RET_CHUNK
RET_PARKED = (N_CHUNKS - 1) // 2
RET_HPS = 2

NA_QROWS = 4
NA_KROWS = 12
NA_QBLK = NA_QROWS * GRID_W
NA_KBLK = NA_KROWS * GRID_W
NA_NBLK = GRID_H // NA_QROWS
NA_HPS = 4
NA_LOOKAHEAD = 4 * NA_HPS
NEG_BIG = -1e30
LOG2E = 1.4426950408889634


def _dot(a, b):
    return jnp.dot(a, b, preferred_element_type=F32)


def _dot_nt(a, b):
    return lax.dot_general(a, b, (((1,), (1,)), ((), ())), preferred_element_type=F32)


def _dot_tn(a, b):
    return lax.dot_general(a, b, (((0,), (0,)), ((), ())), preferred_element_type=F32)


def _silu(x):
    half = 0.5 * x
    return half + half * jnp.tanh(half)


def _mod_kernel(c_ref, w_ref, b_ref, o_ref):
    cond = _silu(c_ref[...]).astype(BF16)
    o_ref[0] = _dot(cond, w_ref[0].astype(BF16)) + b_ref[0]


def _modulation(cvec, mod_w, mod_b):
    tn = 1024
    n = 3 * D_MODEL
    return pl.pallas_call(
        _mod_kernel,
        grid=(DEPTH, n // tn),
        in_specs=[
            pl.BlockSpec((8, D_MODEL), lambda l, j: (0, 0)),
            pl.BlockSpec((1, D_MODEL, tn), lambda l, j: (l, 0, j)),
            pl.BlockSpec((1, 1, tn), lambda l, j: (l, 0, j)),
        ],
        out_specs=pl.BlockSpec((1, 8, tn), lambda l, j: (l, 0, j)),
        out_shape=jax.ShapeDtypeStruct((DEPTH, 8, n), F32),
        compiler_params=pltpu.CompilerParams(
            dimension_semantics=("arbitrary", "arbitrary"), vmem_limit_bytes=VMEM_LIMIT),
        name="modulation",
    )(cvec, mod_w, mod_b.reshape(DEPTH, 1, n))


def _norm_mod_kernel(x_ref, g_ref, sc_ref, sh_ref, o_ref):
    ctx_row = jnp.where(pl.program_id(1) == 0, 0, 1)
    g = g_ref[...]
    gain = [g * (1.0 + sc_ref[0, pl.ds(row, 1), :]) for row in (ctx_row, 1)]
    shift = [sh_ref[0, pl.ds(row, 1), :] for row in (ctx_row, 1)]
    for r0 in range(0, TM_NORM, NORM_SLAB):
        which = 0 if r0 < CTX_LEN else 1
        x = x_ref[0, r0:r0 + NORM_SLAB, :]
        y = x * lax.rsqrt(jnp.mean(x * x, axis=-1, keepdims=True) + EPS)
        o_ref[0, r0:r0 + NORM_SLAB, :] = (y * gain[which] + shift[which]).astype(BF16)


def _norm_mod(xs, g, sc, sh):
    tm = TM_NORM
    return pl.pallas_call(
        _norm_mod_kernel,
        grid=(BATCH, TOK // tm),
        in_specs=[
            pl.BlockSpec((1, tm, D_MODEL), lambda b, m: (b, m, 0)),
            pl.BlockSpec((1, D_MODEL), lambda b, m: (0, 0)),
            pl.BlockSpec((1, 2, D_MODEL), lambda b, m: (b, 0, 0)),
            pl.BlockSpec((1, 2, D_MODEL), lambda b, m: (b, 0, 0)),
        ],
        out_specs=pl.BlockSpec((1, tm, D_MODEL), lambda b, m: (b, m, 0)),
        out_shape=jax.ShapeDtypeStruct((BATCH, TOK, D_MODEL), BF16),
        compiler_params=pltpu.CompilerParams(
            dimension_semantics=("arbitrary", "arbitrary"), vmem_limit_bytes=VMEM_LIMIT),
        name="norm_mod",
    )(xs, g, sc, sh)


def _first_token_tile():
    return (pl.program_id(1) == 0) & (pl.program_id(2) == 0)


def _resident_matmul(h_ref, w_ref, wb_scr):
    @pl.when(_first_token_tile())
    def _():
        wb_scr[...] = w_ref[0].astype(BF16)

    return [(r0, _dot(h_ref[0, r0:r0 + MM_ROWS, :], wb_scr[...])) for r0 in range(0, h_ref.shape[1], MM_ROWS)]


def _store_heads(o_ref, r0, c0, val):
    hw = o_ref.shape[-1]
    rows, w = val.shape
    if w <= hw:
        o_ref[0, c0 // hw, r0:r0 + rows, c0 % hw:c0 % hw + w] = val
    else:
        for k in range(w // hw):
            o_ref[0, c0 // hw + k, r0:r0 + rows, :] = val[:, k * hw:(k + 1) * hw]


def _in_proj_scaled_kernel(h_ref, w_ref, o_ref, wb_scr, *, n_scaled, scale):
    factor = jnp.where(pl.program_id(0) < n_scaled, scale, 1.0)
    for r0, acc in _resident_matmul(h_ref, w_ref, wb_scr):
        _store_heads(o_ref, r0, 0, (acc * factor).astype(BF16))


def _in_proj_silu_kernel(h_ref, w_ref, o_ref, wb_scr):
    for r0, acc in _resident_matmul(h_ref, w_ref, wb_scr):
        _store_heads(o_ref, r0, 0, _silu(acc).astype(BF16))


def _in_proj_rope_kernel(h_ref, w_ref, cos_ref, sin_ref, o_ref, wb_scr):
    scale = jnp.where(pl.program_id(0) < D_MODEL // TN_IN, 1.0, RET_QK_DIM ** -0.5)
    for r0, acc in _resident_matmul(h_ref, w_ref, wb_scr):
        rows = slice(r0, r0 + MM_ROWS)
        for c0 in range(0, TN_IN, LANES):
            t0 = c0 % RET_QK_DIM
            xg = acc[:, c0:c0 + LANES]
            r = xg * cos_ref[rows, t0:t0 + LANES] + pltpu.roll(xg, LANES // 2, 1) * sin_ref[rows, t0:t0 + LANES]
            _store_heads(o_ref, r0, c0, (r * scale).astype(BF16))


def _in_proj(body, name, h, w_all, layer, col0, ncols, head_w, tables=()):
    tile0 = col0 // TN_IN
    tm = TM
    table_spec = pl.BlockSpec((tm, RET_QK_DIM), lambda j, b, m: (m, 0), pipeline_mode=pl.Buffered(1))
    return pl.pallas_call(
        body,
        grid=(ncols // TN_IN, BATCH, TOK // tm),
        in_specs=[
            pl.BlockSpec((1, tm, D_MODEL), lambda j, b, m: (b, m, 0)),
            pl.BlockSpec((1, D_MODEL, TN_IN), lambda j, b, m: (layer, 0, tile0 + j)),
        ] + [table_spec] * len(tables),
        out_specs=pl.BlockSpec((1, TN_IN // head_w, tm, head_w), lambda j, b, m: (b, j, m, 0)),
        out_shape=jax.ShapeDtypeStruct((BATCH, ncols // head_w, TOK, head_w), BF16),
        scratch_shapes=[pltpu.VMEM((D_MODEL, TN_IN), BF16)],
        compiler_params=pltpu.CompilerParams(
            dimension_semantics=("arbitrary", "arbitrary", "arbitrary"), vmem_limit_bytes=VMEM_LIMIT),
        name=name,
    )(h, w_all, *tables)


def _out_proj_kernel(a_ref, w_ref, x_ref, gt_ref, o_ref, wb_scr):
    @pl.when(_first_token_tile())
    def _():
        wb_scr[...] = w_ref[0].astype(BF16)

    for r0 in range(0, TM_OUT, MM_ROWS):
        rows = slice(r0, r0 + MM_ROWS)
        a = jnp.concatenate([a_ref[0, hd, rows, :] for hd in range(a_ref.shape[1])], axis=-1)
        y = _dot(a, wb_scr[...])
        tok = pl.program_id(2) * TM_OUT + r0 + lax.broadcasted_iota(jnp.int32, (MM_ROWS, 1), 0)
        gt = jnp.where(tok < CTX_LEN, gt_ref[0, 0:1, :], gt_ref[0, 1:2, :])
        o_ref[0, rows, :] = x_ref[0, rows, :] + gt * y


def _out_proj(a, w_all, layer, xs, gt):
    n_heads, head_w = a.shape[1], a.shape[3]
    return pl.pallas_call(
        _out_proj_kernel,
        grid=(D_MODEL // TN_OUT, BATCH, TOK // TM_OUT),
        in_specs=[
            pl.BlockSpec((1, n_heads, TM_OUT, head_w), lambda j, b, m: (b, 0, m, 0)),
            pl.BlockSpec((1, D_BRANCH, TN_OUT), lambda j, b, m: (layer, 0, j)),
            pl.BlockSpec((1, TM_OUT, TN_OUT), lambda j, b, m: (b, m, j)),
            pl.BlockSpec((1, 2, TN_OUT), lambda j, b, m: (b, 0, j)),
        ],
        out_specs=pl.BlockSpec((1, TM_OUT, TN_OUT), lambda j, b, m: (b, m, j)),
        out_shape=jax.ShapeDtypeStruct((BATCH, TOK, D_MODEL), F32),
        scratch_shapes=[pltpu.VMEM((D_BRANCH, TN_OUT), BF16)],
        input_output_aliases={2: 0},
        compiler_params=pltpu.CompilerParams(
            dimension_semantics=("arbitrary", "arbitrary", "arbitrary"), vmem_limit_bytes=VMEM_LIMIT),
        name="out_proj",
    )(a, w_all, xs, gt)


def _ret_kernel(q_ref, k_ref, v_ref, g_ref, raw_ref, o_ref, of_scr, ob_scr):
    c = RET_CHUNK
    pos_i = lax.broadcasted_iota(jnp.int32, (c, 1), 0).astype(F32)
    ii = lax.broadcasted_iota(jnp.int32, (c, c), 0)
    jj = lax.broadcasted_iota(jnp.int32, (c, c), 1)
    diff = (ii - jj).astype(F32)

    def decays(lg, reverse):
        d = -diff if reverse else diff
        intra = jnp.where(d >= 0, jnp.exp(lg * jnp.maximum(d, 0.0)), 0.0)
        p = (c - 1.0 - pos_i) if reverse else pos_i
        q_dec = jnp.exp(lg * (p + 1.0))
        k_dec = jnp.exp(lg * (c - 1.0 - p))
        c_dec = jnp.exp(lg * float(c))
        return intra, q_dec, k_dec, c_dec

    def head_decays(hd):
        raw = raw_ref[hd]
        log_g = -(jnp.maximum(raw, 0.0) + jnp.log1p(jnp.exp(-jnp.abs(raw))))
        return decays(log_g[0:1, 0:1], False), decays(log_g[1:2, 0:1], True)

    def rows(chunk):
        return pl.ds(chunk * c, c)

    def scan_chunk(hd, chunk, s, dec):
        intra, q_dec, k_dec, c_dec = dec
        qc = q_ref[hd, rows(chunk), :]
        kc = k_ref[hd, rows(chunk), :]
        vc = v_ref[hd, rows(chunk), :]
        scores = (_dot_nt(qc, kc) * intra).astype(BF16)
        k_scaled = (kc.astype(F32) * k_dec).astype(BF16)
        if s is None:
            return _dot(scores, vc), _dot_tn(k_scaled, vc)
        q_scaled = (qc.astype(F32) * q_dec).astype(BF16)
        return _dot(scores, vc) + _dot(q_scaled, s.astype(BF16)), s * c_dec + _dot_tn(k_scaled, vc)

    def finalize(hd, chunk, tot):
        nrm = tot * lax.rsqrt(jnp.mean(tot * tot, axis=-1, keepdims=True) + EPS)
        o_ref[0, hd, rows(chunk), :] = (g_ref[hd, rows(chunk), :].astype(F32) * nrm).astype(BF16)

    fwd_order = list(range(N_CHUNKS))
    bwd_order = [0] + list(range(N_CHUNKS - 1, 0, -1))
    fwd_slot = {ch: n for n, ch in enumerate(fwd_order[1:1 + RET_PARKED])}
    bwd_slot = {ch: n for n, ch in enumerate(bwd_order[1:1 + RET_PARKED])}
    dec = [head_decays(hd) for hd in range(RET_HPS)]
    s_f = [None] * RET_HPS
    s_b = [None] * RET_HPS
    for cf, cb in zip(fwd_order, bwd_order):
        for hd in range(RET_HPS):
            o_f, s_f[hd] = scan_chunk(hd, cf, s_f[hd], dec[hd][0])
            o_b, s_b[hd] = scan_chunk(hd, cb, s_b[hd], dec[hd][1])
            if cf == cb:
                finalize(hd, cf, o_f + o_b)
                continue
            if cf in fwd_slot:
                of_scr[hd, rows(fwd_slot[cf]), :] = o_f
            else:
                finalize(hd, cf, o_f + ob_scr[hd, rows(bwd_slot[cf]), :])
            if cb in bwd_slot:
                ob_scr[hd, rows(bwd_slot[cb]), :] = o_b
            else:
                finalize(hd, cb, of_scr[hd, rows(fwd_slot[cb]), :] + o_b)


def _ret_core(qk, v, g, raw):
    nblk = RET_HEADS // RET_HPS
    return pl.pallas_call(
        _ret_kernel,
        grid=(BATCH, nblk),
        in_specs=[
            pl.BlockSpec((None, RET_HPS, TOK, RET_QK_DIM), lambda b, h: (b, h, 0, 0)),
            pl.BlockSpec((None, RET_HPS, TOK, RET_QK_DIM), lambda b, h: (b, nblk + h, 0, 0)),
            pl.BlockSpec((None, RET_HPS, TOK, RET_V_DIM), lambda b, h: (b, h, 0, 0)),
            pl.BlockSpec((None, RET_HPS, TOK, RET_V_DIM), lambda b, h: (b, h, 0, 0)),
            pl.BlockSpec((RET_HPS, 2, LANES), lambda b, h: (h, 0, 0)),
        ],
        out_specs=pl.BlockSpec((1, RET_HPS, TOK, RET_V_DIM), lambda b, h: (b, h, 0, 0)),
        out_shape=jax.ShapeDtypeStruct((BATCH, RET_HEADS, TOK, RET_V_DIM), BF16),
        scratch_shapes=[pltpu.VMEM((RET_HPS, RET_PARKED * RET_CHUNK, RET_V_DIM), F32),
                        pltpu.VMEM((RET_HPS, RET_PARKED * RET_CHUNK, RET_V_DIM), F32)],
        compiler_params=pltpu.CompilerParams(
            dimension_semantics=("arbitrary", "arbitrary"), vmem_limit_bytes=VMEM_LIMIT),
        name="ret_core",
    )(qk, qk, v, g, raw)


def _na_key_start(i):
    return int(np.clip(NA_QROWS * i - NA_WIN_H // 2, 0, GRID_H - NA_KROWS))


def _na_row_geometry(i, a, j):
    r = NA_QROWS * i + a
    rs = int(np.clip(r - NA_WIN_H // 2, 0, GRID_H - NA_WIN_H))
    kr = _na_key_start(i) + j
    return rs <= kr < rs + NA_WIN_H, kr - r + NA_WIN_H - 1


NA_KIND_BLOCKS = (0, 1, NA_NBLK - 1)
for _i in range(2, NA_NBLK - 1):
    assert all(_na_row_geometry(_i, _a, _j) == _na_row_geometry(1, _a, _j)
               for _a in range(NA_QROWS) for _j in range(NA_KROWS))


def _rpb_pair_table(rpb):
    mid = NA_WIN_W - 1
    pad = jnp.zeros(rpb.shape[:2] + (LANES - rpb.shape[2],), rpb.dtype)
    centred = jnp.concatenate([rpb[..., mid:], pad, rpb[..., :mid]], axis=-1)
    shifted = jnp.roll(centred, GRID_W, axis=-1)
    lane = jnp.arange(LANES)
    around_64 = (lane >= GRID_W // 2) & (lane < LANES - GRID_W // 2)
    pair = jnp.where(around_64, shifted[:, 1:], centred[:, :-1])
    return jnp.concatenate([pair, jnp.zeros((rpb.shape[0], 2, LANES), rpb.dtype)], axis=1)


def _na_build_bias(pair_ref, bias_scr):
    c = lax.broadcasted_iota(jnp.int32, (GRID_W, LANES), 0)
    lane = lax.broadcasted_iota(jnp.int32, (GRID_W, LANES), 1)
    kc = lane & (GRID_W - 1)
    ws = jnp.clip(c - NA_WIN_W // 2, 0, GRID_W - NA_WIN_W)
    in_cols = (kc >= ws) & (kc < ws + NA_WIN_W)
    first = lane < GRID_W
    masks = {(True, True): in_cols,
             (True, False): in_cols & first,
             (False, True): in_cols & jnp.logical_not(first)}
    toeplitz = {}
    for kind, i in enumerate(NA_KIND_BLOCKS):
        for a in range(NA_QROWS):
            for grp in range(NA_KROWS // 2):
                v0, d0 = _na_row_geometry(i, a, 2 * grp)
                v1, _ = _na_row_geometry(i, a, 2 * grp + 1)
                if v0 or v1:
                    assert 0 <= d0 <= 2 * NA_WIN_H - 3
                    if d0 not in toeplitz:
                        base = jnp.broadcast_to(pair_ref[d0:d0 + 1, :], (GRID_W, LANES))
                        toeplitz[d0] = pltpu.roll(base, 0, 1, stride=1, stride_axis=0) * LOG2E
                    tile = jnp.where(masks[(v0, v1)], toeplitz[d0], NEG_BIG)
                else:
                    tile = jnp.full((GRID_W, LANES), NEG_BIG, F32)
                bias_scr[kind, a * GRID_W:(a + 1) * GRID_W, grp * LANES:(grp + 1) * LANES] = tile


def _na_kernel(q_ref, k_ref, v_ref, g_ref, pair_ref, o_ref, bias_ref, v1_scr, kt_scr):
    d = NA_HEAD_DIM

    @pl.when(pl.program_id(1) == 0)
    def _():
        for hd in range(NA_HPS):
            _na_build_bias(pair_ref.at[hd], bias_ref.at[hd])
            v1_scr[hd, :, d:] = jnp.ones((TOK, d), BF16)

    for hd in range(NA_HPS):
        v1_scr[hd, :, :d] = v_ref[hd]
        kt_scr[hd] = k_ref[hd].astype(F32).T.astype(BF16)

    def finish(hd, o1, r0, rows):
        gate = g_ref[hd, r0:r0 + rows, :].astype(F32)
        o_ref[0, hd, r0:r0 + rows, :] = (gate * (o1[:, :d] / o1[:, d:])).astype(BF16)

    def scores(job):
        hd, i = job
        kc = kt_scr[hd, :, 0:CTX_LEN]
        if i is None:
            return None, _dot(q_ref[hd, 0:CTX_LEN, :], kc)
        kind = 0 if i == 0 else (2 if i == NA_NBLK - 1 else 1)
        r0 = CTX_LEN + i * NA_QBLK
        k0 = CTX_LEN + _na_key_start(i) * GRID_W
        q = q_ref[hd, r0:r0 + NA_QBLK, :]
        return _dot(q, kt_scr[hd, :, k0:k0 + NA_KBLK]) + bias_ref[hd, kind], _dot(q, kc)

    def attend(job, s_loc, s_ctx):
        hd, i = job
        vc = v1_scr[hd, 0:CTX_LEN, :]
        m = jnp.max(s_ctx, axis=-1, keepdims=True)
        if i is None:
            finish(hd, _dot(jnp.exp2(s_ctx - m).astype(BF16), vc), 0, CTX_LEN)
            return
        r0 = CTX_LEN + i * NA_QBLK
        k0 = CTX_LEN + _na_key_start(i) * GRID_W
        m = jnp.maximum(jnp.max(s_loc, axis=-1, keepdims=True), m)
        p_loc = jnp.exp2(s_loc - m).astype(BF16)
        p_ctx = jnp.exp2(s_ctx - m).astype(BF16)
        finish(hd, _dot(p_loc, v1_scr[hd, k0:k0 + NA_KBLK, :]) + _dot(p_ctx, vc), r0, NA_QBLK)

    blocks = list(range(NA_NBLK))
    blocks.insert(NA_NBLK // 2, None)
    order = [(hd, i) for i in blocks for hd in range(NA_HPS)]
    pending = [scores(job) for job in order[:NA_LOOKAHEAD]]
    for n, job in enumerate(order):
        s_loc, s_ctx = pending.pop(0)
        if n + NA_LOOKAHEAD < len(order):
            pending.append(scores(order[n + NA_LOOKAHEAD]))
        attend(job, s_loc, s_ctx)


def _na_core(qkv, g, rpb_pairs):
    nblk = NA_HEADS // NA_HPS
    heads = (None, NA_HPS, TOK, NA_HEAD_DIM)
    return pl.pallas_call(
        _na_kernel,
        grid=(nblk, BATCH),
        in_specs=[
            pl.BlockSpec(heads, lambda h, b: (b, h, 0, 0)),
            pl.BlockSpec(heads, lambda h, b: (b, nblk + h, 0, 0)),
            pl.BlockSpec(heads, lambda h, b: (b, 2 * nblk + h, 0, 0)),
            pl.BlockSpec(heads, lambda h, b: (b, h, 0, 0)),
            pl.BlockSpec((NA_HPS, 2 * NA_WIN_H, LANES), lambda h, b: (h, 0, 0)),
        ],
        out_specs=pl.BlockSpec((1, NA_HPS, TOK, NA_HEAD_DIM), lambda h, b: (b, h, 0, 0)),
        out_shape=jax.ShapeDtypeStruct((BATCH, NA_HEADS, TOK, NA_HEAD_DIM), BF16),
        scratch_shapes=[pltpu.VMEM((NA_HPS, len(NA_KIND_BLOCKS), NA_QBLK, NA_KBLK), F32),
                        pltpu.VMEM((NA_HPS, TOK, 2 * NA_HEAD_DIM), BF16),
                        pltpu.VMEM((NA_HPS, NA_HEAD_DIM, TOK), BF16)],
        compiler_params=pltpu.CompilerParams(
            dimension_semantics=("arbitrary", "arbitrary"), vmem_limit_bytes=VMEM_LIMIT),
        name="na_core",
    )(qkv, qkv, qkv, g, rpb_pairs)


def _final_norm_kernel(x_ref, g_ref, o_ref):
    x = x_ref[0]
    o_ref[0] = x * lax.rsqrt(jnp.mean(x * x, axis=-1, keepdims=True) + EPS) * g_ref[...]


def _final_norm(xs, g):
    tm = CTX_LEN
    return pl.pallas_call(
        _final_norm_kernel,
        grid=(BATCH, SEQ // tm),
        in_specs=[
            pl.BlockSpec((1, tm, D_MODEL), lambda b, i: (b, i + CTX_LEN // tm, 0)),
            pl.BlockSpec((1, D_MODEL), lambda b, i: (0, 0)),
        ],
        out_specs=pl.BlockSpec((1, tm, D_MODEL), lambda b, i: (b, i, 0)),
        out_shape=jax.ShapeDtypeStruct((BATCH, SEQ, D_MODEL), F32),
        compiler_params=pltpu.CompilerParams(
            dimension_semantics=("arbitrary", "arbitrary"), vmem_limit_bytes=VMEM_LIMIT),
        name="final_norm",
    )(xs, g)


def _rope_tables():
    quarter = RET_QK_DIM // 4
    f32 = np.float32
    freqs = f32(ROPE_BASE) ** (-np.arange(quarter, dtype=f32) / f32(quarter))
    t = np.arange(SEQ)
    ang_r = (t // GRID_W).astype(f32)[:, None] * freqs[None, :]
    ang_c = (t % GRID_W).astype(f32)[:, None] * freqs[None, :]
    cos = np.concatenate([np.cos(ang_r)] * 2 + [np.cos(ang_c)] * 2, axis=-1)
    sin = np.concatenate([-np.sin(ang_r), np.sin(ang_r), -np.sin(ang_c), np.sin(ang_c)], axis=-1)
    cos = np.concatenate([np.ones((CTX_LEN, RET_QK_DIM), f32), cos], axis=0)
    sin = np.concatenate([np.zeros((CTX_LEN, RET_QK_DIM), f32), sin], axis=0)
    return jnp.asarray(cos, F32), jnp.asarray(sin, F32)


def _per_batch(m):
    ctx_row = jnp.broadcast_to(m[BATCH][None, None, :], (BATCH, 1, m.shape[-1]))
    return jnp.concatenate([ctx_row, m[:BATCH][:, None, :]], axis=1)


def kernel(x, c, ctx, c_ctx, mod_w, mod_b, norm_g, ret_w_in, ret_decay_fwd, ret_decay_bwd, ret_w_out,
           na_w_in, na_rpb, na_w_out, final_g):
    xs = jnp.concatenate([ctx, x], axis=1)
    cvec = jnp.concatenate([c, c_ctx[None], jnp.zeros((8 - BATCH - 1, D_MODEL), F32)], axis=0)
    mods = _modulation(cvec, mod_w, mod_b)
    cos_t, sin_t = _rope_tables()

    for l in range(DEPTH):
        j = l // 2
        sh = _per_batch(mods[l, :, :D_MODEL])
        sc = _per_batch(mods[l, :, D_MODEL:2 * D_MODEL])
        gt = _per_batch(mods[l, :, 2 * D_MODEL:])
        h = _norm_mod(xs, norm_g[l][None, :], sc, sh)
        if l % 2 == 0:
            qk = _in_proj(_in_proj_rope_kernel, "in_proj_ret_qk", h, ret_w_in, j, 0, 2 * D_MODEL, RET_QK_DIM,
                          (cos_t, sin_t))
            plain = functools.partial(_in_proj_scaled_kernel, n_scaled=0, scale=1.0)
            v = _in_proj(plain, "in_proj_ret_v", h, ret_w_in, j, 2 * D_MODEL, D_BRANCH, RET_V_DIM)
            g = _in_proj(_in_proj_silu_kernel, "in_proj_ret_g", h, ret_w_in, j, 2 * D_MODEL + D_BRANCH, D_BRANCH,
                         RET_V_DIM)
            raw = jnp.stack([ret_decay_fwd[j], ret_decay_bwd[j]], axis=1)
            raw = jnp.broadcast_to(raw[:, :, None], (RET_HEADS, 2, LANES))
            a = _ret_core(qk, v, g, raw)
            xs = _out_proj(a, ret_w_out, j, xs, gt)
        else:
            scaled_q = functools.partial(_in_proj_scaled_kernel, n_scaled=D_BRANCH // TN_IN,
                                         scale=NA_HEAD_DIM ** -0.5 * LOG2E)
            qkv = _in_proj(scaled_q, "in_proj_na_qkv", h, na_w_in, j, 0, 3 * D_BRANCH, NA_HEAD_DIM)
            g = _in_proj(_in_proj_silu_kernel, "in_proj_na_g", h, na_w_in, j, 3 * D_BRANCH, D_BRANCH, NA_HEAD_DIM)
            a = _na_core(qkv, g, _rpb_pair_table(na_rpb[j]))
            xs = _out_proj(a, na_w_out, j, xs, gt)
    return _final_norm(xs, final_g[None, :])
```
